```python
import math
import jax, jax.numpy as jnp
from jax import lax
import numpy as np

D_MODEL = 1024
BATCH = 4
SEQ = 8192
DEPTH = 2

CHUNK = 64
Q_BLOCK = 128
N_MIXERS = 2
N_HEADS = 8
HEAD_DIM = D_MODEL // (2 * N_HEADS)
V_HEAD_DIM = 2 * HEAD_DIM
CONV_WIDTH = 31
D_FF = ((-(-8 * D_MODEL // 3) + 255) // 256) * 256
PLE_DIM = 256
N_ATTN = (DEPTH + 1) // 2
N_CONV = DEPTH // 2
NORM_EPS = 1e-6

kernel_name = "hybrid_diffattn_conformer_conv_stream_encoder"


def lambda_init_fn(layer_idx):
    return 0.8 - 0.6 * math.exp(-0.3 * layer_idx)


def rms_norm(x, g):
    xf = x.astype(jnp.float32)
    y = xf * lax.rsqrt(jnp.mean(xf * xf, axis=-1, keepdims=True) + NORM_EPS)
    return (y * g.astype(jnp.float32)).astype(x.dtype)


def layer_norm(x, g, b):
    xf = x.astype(jnp.float32)
    mu = jnp.mean(xf, axis=-1, keepdims=True)
    xc = xf - mu
    y = xc * lax.rsqrt(jnp.mean(xc * xc, axis=-1, keepdims=True) + NORM_EPS)
    return (y * g.astype(jnp.float32) + b.astype(jnp.float32)).astype(x.dtype)


def diff_attention(xn, w_qkv, lq1, lk1, lq2, lk2, subln_g, w_o, lambda_init):
    B, S, _ = xn.shape
    qkv = xn @ w_qkv
    q, k, v = jnp.split(qkv, 3, axis=-1)
    q = q.reshape(B, S, 2 * N_HEADS, HEAD_DIM)
    k = k.reshape(B, S, 2 * N_HEADS, HEAD_DIM)
    v = v.reshape(B, S, N_HEADS, V_HEAD_DIM)
    f32 = jnp.float32
    lam = (jnp.exp(jnp.sum(lq1.astype(f32) * lk1.astype(f32)))
           - jnp.exp(jnp.sum(lq2.astype(f32) * lk2.astype(f32))) + lambda_init)
    n_blk = S // Q_BLOCK
    q_blocks = q.reshape(B, n_blk, Q_BLOCK, 2 * N_HEADS, HEAD_DIM).transpose(1, 0, 2, 3, 4)
    k_chunk = jnp.arange(S) // CHUNK
    scale = HEAD_DIM ** -0.5

    def one_block(args):
        blk, qb = args
        s = jnp.einsum('bqhd,bkhd->bhqk', qb, k).astype(f32) * scale
        q_chunk = (blk * Q_BLOCK + jnp.arange(Q_BLOCK)) // CHUNK
        mask = k_chunk[None, :] <= q_chunk[:, None]
        s = jnp.where(mask[None, None], s, -jnp.inf)
        a = jax.nn.softmax(s, axis=-1).reshape(B, N_HEADS, 2, Q_BLOCK, S)
        a = a[:, :, 0] - lam * a[:, :, 1]
        return jnp.einsum('bhqk,bkhe->bqhe', a.astype(v.dtype), v)

    o = lax.map(one_block, (jnp.arange(n_blk), q_blocks))
    o = o.transpose(1, 0, 2, 3, 4).reshape(B, S, N_HEADS, V_HEAD_DIM)
    o = rms_norm(o, subln_g) * (1.0 - lambda_init)
    return o.reshape(B, S, D_MODEL) @ w_o


def conformer_conv(xn, pw1_w, pw1_b, dw_w, dw_b, ln_g, ln_b, pw2_w, pw2_b):
    u = xn @ pw1_w + pw1_b
    a, g = jnp.split(u, 2, axis=-1)
    u = a * jax.nn.sigmoid(g)
    u = lax.conv_general_dilated(
        u, dw_w.reshape(CONV_WIDTH, 1, D_MODEL).astype(u.dtype),
        window_strides=(1,), padding=[(CONV_WIDTH - 1, 0)],
        dimension_numbers=('NWC', 'WIO', 'NWC'),
        feature_group_count=D_MODEL) + dw_b
    u = jax.nn.silu(layer_norm(u, ln_g, ln_b))
    return u @ pw2_w + pw2_b


def swiglu(xn, w_gate, w_up, w_down):
    return (jax.nn.silu(xn @ w_gate) * (xn @ w_up)) @ w_down


def setup_inputs(seed: int = 0) -> dict:
    key = jax.random.key(seed)
    ks = jax.random.split(key, 32)
    f32 = jnp.float32
    D = D_MODEL

    def nrm(k, shape, scale):
        return jax.random.normal(k, shape, f32) * scale

    def gain(k, shape):
        return 1.0 + 0.01 * jax.random.normal(k, shape, f32)

    return {
        "x": nrm(ks[0], (BATCH, SEQ, D), 1.0),
        "p": nrm(ks[1], (DEPTH, BATCH, SEQ, PLE_DIM), 1.0),
        "attn_norm_g": gain(ks[2], (N_ATTN, D)),
        "w_qkv": nrm(ks[3], (N_ATTN, D, 3 * D), D ** -0.5),
        "lambda_q1": nrm(ks[4], (N_ATTN, HEAD_DIM), 0.1),
        "lambda_k1": nrm(ks[5], (N_ATTN, HEAD_DIM), 0.1),
        "lambda_q2": nrm(ks[6], (N_ATTN, HEAD_DIM), 0.1),
        "lambda_k2": nrm(ks[7], (N_ATTN, HEAD_DIM), 0.1),
        "subln_g": gain(ks[8], (N_ATTN, V_HEAD_DIM)),
        "w_o": nrm(ks[9], (N_ATTN, D, D), D ** -0.5),
        "conv_norm_g": gain(ks[10], (N_CONV, D)),
        "conv_pw1_w": nrm(ks[11], (N_CONV, D, 2 * D), D ** -0.5),
        "conv_pw1_b": nrm(ks[12], (N_CONV, 2 * D), 0.01),
        "conv_dw_w": nrm(ks[13], (N_CONV, CONV_WIDTH, D), CONV_WIDTH ** -0.5),
        "conv_dw_b": nrm(ks[14], (N_CONV, D), 0.01),
        "conv_ln_g": gain(ks[15], (N_CONV, D)),
        "conv_ln_b": nrm(ks[16], (N_CONV, D), 0.01),
        "conv_pw2_w": nrm(ks[17], (N_CONV, D, D), D ** -0.5),
        "conv_pw2_b": nrm(ks[18], (N_CONV, D), 0.01),
        "ffn_norm_g": gain(ks[19], (DEPTH, D)),
        "w_gate": nrm(ks[20], (DEPTH, D, D_FF), D ** -0.5),
        "w_up": nrm(ks[21], (DEPTH, D, D_FF), D ** -0.5),
        "w_down": nrm(ks[22], (DEPTH, D_FF, D), D_FF ** -0.5),
        "ple_norm_g": gain(ks[23], (DEPTH, D)),
        "w_ple_gate": nrm(ks[24], (DEPTH, D, D), D ** -0.5),
        "w_ple_proj": nrm(ks[25], (DEPTH, PLE_DIM, D), PLE_DIM ** -0.5),
        "final_norm_g": gain(ks[26], (D,)),
    }


def reference(x, p, attn_norm_g, w_qkv, lambda_q1, lambda_k1, lambda_q2, lambda_k2,
              subln_g, w_o, conv_norm_g, conv_pw1_w, conv_pw1_b, conv_dw_w, conv_dw_b,
              conv_ln_g, conv_ln_b, conv_pw2_w, conv_pw2_b, ffn_norm_g, w_gate, w_up,
              w_down, ple_norm_g, w_ple_gate, w_ple_proj, final_norm_g):
    h = x
    for i in range(DEPTH):
        j = i // N_MIXERS
        if i % N_MIXERS == 0:
            h = h + diff_attention(rms_norm(h, attn_norm_g[j]), w_qkv[j],
                                   lambda_q1[j], lambda_k1[j], lambda_q2[j], lambda_k2[j],
                                   subln_g[j], w_o[j], lambda_init_fn(i))
        else:
            h = h + conformer_conv(rms_norm(h, conv_norm_g[j]), conv_pw1_w[j], conv_pw1_b[j],
                                   conv_dw_w[j], conv_dw_b[j], conv_ln_g[j], conv_ln_b[j],
                                   conv_pw2_w[j], conv_pw2_b[j])
        h = h + swiglu(rms_norm(h, ffn_norm_g[i]), w_gate[i], w_up[i], w_down[i])
        gate = jax.nn.sigmoid(rms_norm(h, ple_norm_g[i]) @ w_ple_gate[i])
        h = h + (p[i] @ w_ple_proj[i]) * gate
    return rms_norm(h, final_norm_g)
```

```python
import functools
import math

import jax
import jax.numpy as jnp
from jax import lax
from jax.experimental import pallas as pl
from jax.experimental.pallas import tpu as pltpu

F32 = jnp.float32
BF16 = jnp.bfloat16

NORM_EPS = 1e-6
CHUNK = 64
N_HEADS = 8
HEAD_DIM = 64
V_HEAD_DIM = 2 * HEAD_DIM
CONV_WIDTH = 31
CONV_HALO = 32
MASK_VALUE = -1e30

ROW_TILE = 512
ATTN_TILE = 256
CONV_TILE = 256
CONV_ROWS = 32
COL_CHUNK = 512
FF_CHUNK = 256
VMEM_LIMIT = 56 * 1024 * 1024


def _params(n_axes):
    return pltpu.CompilerParams(dimension_semantics=("arbitrary",) * n_axes,
                                vmem_limit_bytes=VMEM_LIMIT)


def _rms(x, g):
    return x * lax.rsqrt(jnp.mean(x * x, axis=-1, keepdims=True) + NORM_EPS) * g


def _dot(a, b):
    return jnp.dot(a, b, preferred_element_type=F32)


def _const_spec(shape):
    return pl.BlockSpec(shape, lambda *_: (0,) * len(shape))


def _norm_proj_kernel(x_ref, g_ref, w_ref, o_ref):
    xn = _rms(x_ref[...], g_ref[...]).astype(BF16)
    n = o_ref.shape[1]
    for c in range(0, n, COL_CHUNK):
        o_ref[:, c:c + COL_CHUNK] = _dot(xn, w_ref[:, c:c + COL_CHUNK]).astype(o_ref.dtype)


def _norm_proj(h, g, w):
    t, d = h.shape
    n = w.shape[1]
    return pl.pallas_call(
        _norm_proj_kernel,
        grid=(t // ROW_TILE,),
        in_specs=[pl.BlockSpec((ROW_TILE, d), lambda i: (i, 0)),
                  _const_spec((1, d)), _const_spec((d, n))],
        out_specs=pl.BlockSpec((ROW_TILE, n), lambda i: (i, 0)),
        out_shape=jax.ShapeDtypeStruct((t, n), BF16),
        compiler_params=_params(1),
        name="norm_proj",
    )(h, g.reshape(1, d), w.astype(BF16))


def _norm_glu_kernel(x_ref, g_ref, w_ref, b_ref, o_ref):
    xn = _rms(x_ref[...], g_ref[...]).astype(BF16)
    n = o_ref.shape[1]
    for c in range(0, n, COL_CHUNK):
        a = _dot(xn, w_ref[:, c:c + COL_CHUNK]) + b_ref[:, c:c + COL_CHUNK]
        gt = _dot(xn, w_ref[:, n + c:n + c + COL_CHUNK]) + b_ref[:, n + c:n + c + COL_CHUNK]
        o_ref[:, c:c + COL_CHUNK] = a * jax.nn.sigmoid(gt)


def _norm_glu(h, g, w, b):
    t, d = h.shape
    n = w.shape[1] // 2
    return pl.pallas_call(
        _norm_glu_kernel,
        grid=(t // ROW_TILE,),
        in_specs=[pl.BlockSpec((ROW_TILE, d), lambda i: (i, 0)),
                  _const_spec((1, d)), _const_spec((d, 2 * n)), _const_spec((1, 2 * n))],
        out_specs=pl.BlockSpec((ROW_TILE, n), lambda i: (i, 0)),
        out_shape=jax.ShapeDtypeStruct((t, n), F32),
        compiler_params=_params(1),
        name="norm_glu",
    )(h, g.reshape(1, d), w.astype(BF16), b.reshape(1, 2 * n))


def _attn_kernel(q_ref, k_ref, v_ref, lq1_ref, lk1_ref, lq2_ref, lk2_ref, sg_ref, o_ref,
                 *, lambda_init):
    tile = q_ref.shape[1]
    qi = pl.program_id(2)
    q = q_ref[0] * (HEAD_DIM ** -0.5)

    def step(start, carry, mask):
        kblk = k_ref[0, pl.ds(start, tile), :]
        vblk = v_ref[0, pl.ds(start, tile), :]
        out = []
        for j in range(2):
            m, l, acc = carry[j]
            sl = slice(j * HEAD_DIM, (j + 1) * HEAD_DIM)
            s = lax.dot_general(q[:, sl], kblk[:, sl], (((1,), (1,)), ((), ())),
                                preferred_element_type=F32)
            if mask is not None:
                s = jnp.where(mask, s, MASK_VALUE)
            m_new = jnp.maximum(m, jnp.max(s, axis=-1, keepdims=True))
            alpha = jnp.exp(m - m_new)
            e = jnp.exp(s - m_new)
            l_new = alpha * l + jnp.sum(e, axis=-1, keepdims=True)
            acc_new = alpha * acc + _dot(e.astype(BF16), vblk)
            out.append((m_new, l_new, acc_new))
        return tuple(out)

    init = tuple((jnp.full((tile, 1), MASK_VALUE, F32), jnp.zeros((tile, 1), F32),
                  jnp.zeros((tile, V_HEAD_DIM), F32)) for _ in range(2))

    def body(ki, carry):
        return step(pl.multiple_of(ki * tile, tile), carry, None)

    carry = lax.fori_loop(0, qi, body, init)
    rows = lax.broadcasted_iota(jnp.int32, (tile, tile), 0) // CHUNK
    cols = lax.broadcasted_iota(jnp.int32, (tile, tile), 1) // CHUNK
    (_, l0, acc0), (_, l1, acc1) = step(pl.multiple_of(qi * tile, tile), carry, cols <= rows)

    lam = (jnp.exp(jnp.sum(lq1_ref[...] * lk1_ref[...], keepdims=True))
           - jnp.exp(jnp.sum(lq2_ref[...] * lk2_ref[...], keepdims=True)) + lambda_init)
    o = acc0 / l0 - lam * (acc1 / l1)
    o = _rms(o, sg_ref[...]) * (1.0 - lambda_init)
    o_ref[0] = o.astype(o_ref.dtype)


def _attention(qkv, lq1, lk1, lq2, lk2, subln_g, lambda_init):
    b, s, _ = qkv.shape
    d = N_HEADS * V_HEAD_DIM
    vec = lambda a: a.reshape(1, -1)
    return pl.pallas_call(
        functools.partial(_attn_kernel, lambda_init=lambda_init),
        grid=(b, N_HEADS, s // ATTN_TILE),
        in_specs=[pl.BlockSpec((1, ATTN_TILE, V_HEAD_DIM), lambda bi, h, i: (bi, i, h)),
                  pl.BlockSpec((1, s, V_HEAD_DIM), lambda bi, h, i: (bi, 0, N_HEADS + h)),
                  pl.BlockSpec((1, s, V_HEAD_DIM), lambda bi, h, i: (bi, 0, 2 * N_HEADS + h)),
                  _const_spec((1, HEAD_DIM)), _const_spec((1, HEAD_DIM)),
                  _const_spec((1, HEAD_DIM)), _const_spec((1, HEAD_DIM)),
                  _const_spec((1, V_HEAD_DIM))],
        out_specs=pl.BlockSpec((1, ATTN_TILE, V_HEAD_DIM), lambda bi, h, i: (bi, i, h)),
        out_shape=jax.ShapeDtypeStruct((b, s, d), BF16),
        compiler_params=_params(3),
        name="diff_attention",
    )(qkv, qkv, qkv, vec(lq1), vec(lk1), vec(lq2), vec(lk2), vec(subln_g))


def _proj_res_kernel(h_ref, a_ref, w_ref, o_ref):
    n = o_ref.shape[1]
    a = a_ref[...]
    for c in range(0, n, COL_CHUNK):
        o_ref[:, c:c + COL_CHUNK] = h_ref[:, c:c + COL_CHUNK] + _dot(a, w_ref[:, c:c + COL_CHUNK])


def _proj_res(h, a, w):
    t, d = h.shape
    return pl.pallas_call(
        _proj_res_kernel,
        grid=(t // ROW_TILE,),
        in_specs=[pl.BlockSpec((ROW_TILE, d), lambda i: (i, 0)),
                  pl.BlockSpec((ROW_TILE, d), lambda i: (i, 0)),
                  _const_spec((d, d))],
        out_specs=pl.BlockSpec((ROW_TILE, d), lambda i: (i, 0)),
        out_shape=jax.ShapeDtypeStruct((t, d), F32),
        compiler_params=_params(1),
        name="proj_res",
    )(h, a, w.astype(BF16))


def _ffn_kernel(h_ref, g_ref, wg_ref, wu_ref, wd_ref, o_ref):
    x = h_ref[...]
    xn = _rms(x, g_ref[...]).astype(BF16)
    o_ref[...] = x
    for c in range(0, wg_ref.shape[1], FF_CHUNK):
        gate = _dot(xn, wg_ref[:, c:c + FF_CHUNK])
        up = _dot(xn, wu_ref[:, c:c + FF_CHUNK])
        act = (gate * jax.nn.sigmoid(gate) * up).astype(BF16)
        o_ref[...] += _dot(act, wd_ref[c:c + FF_CHUNK, :])


def _ffn(h, g, w_gate, w_up, w_down):
    t, d = h.shape
    f = w_gate.shape[1]
    single = pl.Buffered(1)
    return pl.pallas_call(
        _ffn_kernel,
        grid=(t // ROW_TILE,),
        in_specs=[pl.BlockSpec((ROW_TILE, d), lambda i: (i, 0)),
                  _const_spec((1, d)),
                  pl.BlockSpec((d, f), lambda i: (0, 0), pipeline_mode=single),
                  pl.BlockSpec((d, f), lambda i: (0, 0), pipeline_mode=single),
                  pl.BlockSpec((f, d), lambda i: (0, 0), pipeline_mode=single)],
        out_specs=pl.BlockSpec((ROW_TILE, d), lambda i: (i, 0)),
        out_shape=jax.ShapeDtypeStruct((t, d), F32),
        compiler_params=_params(1),
        name="swiglu_ffn",
    )(h, g.reshape(1, d), w_gate.astype(BF16), w_up.astype(BF16), w_down.astype(BF16))


def _ple_kernel(h_ref, p_ref, g_ref, wgate_ref, wproj_ref, fg_ref, o_ref, *, final_norm):
    x = h_ref[...]
    xn = _rms(x, g_ref[...]).astype(BF16)
    gate = jax.nn.sigmoid(_dot(xn, wgate_ref[...]))
    y = x + _dot(p_ref[...].astype(BF16), wproj_ref[...]) * gate
    if final_norm:
        y = _rms(y, fg_ref[...])
    o_ref[...] = y


def _ple(h, p, g, w_gate, w_proj, final_g, final_norm):
    t, d = h.shape
    e = p.shape[1]
    return pl.pallas_call(
        functools.partial(_ple_kernel, final_norm=final_norm),
        grid=(t // ROW_TILE,),
        in_specs=[pl.BlockSpec((ROW_TILE, d), lambda i: (i, 0)),
                  pl.BlockSpec((ROW_TILE, e), lambda i: (i, 0)),
                  _const_spec((1, d)), _const_spec((d, d)), _const_spec((e, d)),
                  _const_spec((1, d))],
        out_specs=pl.BlockSpec((ROW_TILE, d), lambda i: (i, 0)),
        out_shape=jax.ShapeDtypeStruct((t, d), F32),
        compiler_params=_params(1),
        name="ple",
    )(h, p, g.reshape(1, d), w_gate.astype(BF16), w_proj.astype(BF16), final_g.reshape(1, d))


def _conv_kernel(u_ref, halo_ref, h_ref, dw_ref, dwb_ref, lng_ref, lnb_ref, w_ref, b_ref,
                 o_ref, ext_ref, y_ref):
    tile = u_ref.shape[1]
    first = pl.program_id(1) == 0
    ext_ref[0:CONV_HALO, :] = jnp.where(first, 0.0, halo_ref[0])
    ext_ref[CONV_HALO:, :] = u_ref[0]
    lead = CONV_HALO - (CONV_WIDTH - 1)
    for r in range(0, tile, CONV_ROWS):
        acc = jnp.zeros((CONV_ROWS, u_ref.shape[2]), F32) + dwb_ref[...]
        for k in range(CONV_WIDTH):
            acc = acc + ext_ref[r + lead + k:r + lead + k + CONV_ROWS, :] * dw_ref[k:k + 1, :]
        y_ref[r:r + CONV_ROWS, :] = acc
    y = y_ref[...]
    mu = jnp.mean(y, axis=-1, keepdims=True)
    yc = y - mu
    y = yc * lax.rsqrt(jnp.mean(yc * yc, axis=-1, keepdims=True) + NORM_EPS) * lng_ref[...] + lnb_ref[...]
    y = (y * jax.nn.sigmoid(y)).astype(BF16)
    o_ref[0] = h_ref[0] + _dot(y, w_ref[...]) + b_ref[...]


def _conv_tail(u, h, dw_w, dw_b, ln_g, ln_b, pw2_w, pw2_b):
    b, s, d = u.shape
    per_tile = CONV_TILE // CONV_HALO
    vec = lambda a: a.reshape(1, d)
    return pl.pallas_call(
        _conv_kernel,
        grid=(b, s // CONV_TILE),
        in_specs=[pl.BlockSpec((1, CONV_TILE, d), lambda bi, i: (bi, i, 0)),
                  pl.BlockSpec((1, CONV_HALO, d),
                               lambda bi, i: (bi, jnp.maximum(i * per_tile - 1, 0), 0)),
                  pl.BlockSpec((1, CONV_TILE, d), lambda bi, i: (bi, i, 0)),
                  _const_spec((CONV_WIDTH, d)), _const_spec((1, d)), _const_spec((1, d)),
                  _const_spec((1, d)), _const_spec((d, d)), _const_spec((1, d))],
        out_specs=pl.BlockSpec((1, CONV_TILE, d), lambda bi, i: (bi, i, 0)),
        out_shape=jax.ShapeDtypeStruct((b, s, d), F32),
        scratch_shapes=[pltpu.VMEM((CONV_TILE + CONV_HALO, d), F32),
                        pltpu.VMEM((CONV_TILE, d), F32)],
        compiler_params=_params(2),
        name="conv_tail",
    )(u, u, h, dw_w, vec(dw_b), vec(ln_g), vec(ln_b), pw2_w.astype(BF16), vec(pw2_b))


def _lambda_init(layer_idx):
    return 0.8 - 0.6 * math.exp(-0.3 * layer_idx)


def kernel(x, p, attn_norm_g, w_qkv, lambda_q1, lambda_k1, lambda_q2, lambda_k2, subln_g, w_o, conv_norm_g, conv_pw1_w, conv_pw1_b, conv_dw_w, conv_dw_b, conv_ln_g, conv_ln_b, conv_pw2_w, conv_pw2_b, ffn_norm_g, w_gate, w_up, w_down, ple_norm_g, w_ple_gate, w_ple_proj, final_norm_g):
    b, s, d = x.shape
    depth = p.shape[0]
    t = b * s
    assert d == N_HEADS * V_HEAD_DIM and t % ROW_TILE == 0
    assert s % ATTN_TILE == 0 and s % CONV_TILE == 0 and ATTN_TILE % CHUNK == 0
    h = x.reshape(t, d)
    for i in range(depth):
        j = i // 2
        if i % 2 == 0:
            qkv = _norm_proj(h, attn_norm_g[j], w_qkv[j])
            o = _attention(qkv.reshape(b, s, 3 * d), lambda_q1[j], lambda_k1[j], lambda_q2[j],
                           lambda_k2[j], subln_g[j], _lambda_init(i))
            h = _proj_res(h, o.reshape(t, d), w_o[j])
        else:
            u = _norm_glu(h, conv_norm_g[j], conv_pw1_w[j], conv_pw1_b[j])
            h = _conv_tail(u.reshape(b, s, d), h.reshape(b, s, d), conv_dw_w[j], conv_dw_b[j],
                           conv_ln_g[j], conv_ln_b[j], conv_pw2_w[j], conv_pw2_b[j]).reshape(t, d)
        h = _ffn(h, ffn_norm_g[i], w_gate[i], w_up[i], w_down[i])
        h = _ple(h, p[i].reshape(t, -1), ple_norm_g[i], w_ple_gate[i], w_ple_proj[i],
                 final_norm_g, final_norm=(i == depth - 1))
    return h.reshape(b, s, d)
```

```python
import functools
import math

import jax
import jax.numpy as jnp
from jax import lax
from jax.experimental import pallas as pl
from jax.experimental.pallas import tpu as pltpu

F32 = jnp.float32
BF16 = jnp.bfloat16

NORM_EPS = 1e-6
CHUNK = 64
N_HEADS = 8
HEAD_DIM = 64
V_HEAD_DIM = 2 * HEAD_DIM
CONV_WIDTH = 31
CONV_HALO = 32
MASK_VALUE = -1e30

SCORE_SCALE = HEAD_DIM ** -0.5 * math.log2(math.e)
SUM_ROWS = 16
SUBLANES = 8
MXU_COLS = 256

ROW_TILE = 512
ATTN_Q = 512
ATTN_K = 512
CONV_TILE = 256
CONV_ROWS = 32
COL_CHUNK = 512
FF_CHUNK = 256
VMEM_LIMIT = 56 * 1024 * 1024


def _params(n_axes):
    return pltpu.CompilerParams(dimension_semantics=("arbitrary",) * n_axes,
                                vmem_limit_bytes=VMEM_LIMIT)


def _rms(x, g):
    return x * lax.rsqrt(jnp.mean(x * x, axis=-1, keepdims=True) + NORM_EPS) * g


def _dot(a, b):
    return jnp.dot(a, b, preferred_element_type=F32)


def _const_spec(shape):
    return pl.BlockSpec(shape, lambda *_: (0,) * len(shape))


def _norm_qkv_kernel(x_ref, g_ref, wqk_ref, wvt_ref, qk_ref, vt_ref):
    xn = _rms(x_ref[...], g_ref[...]).astype(BF16)
    d = vt_ref.shape[1]
    for c in range(0, 2 * d, COL_CHUNK):
        y = _dot(xn, wqk_ref[:, c:c + COL_CHUNK])
        if c < d:
            y = y * SCORE_SCALE
        qk_ref[:, c:c + COL_CHUNK] = y.astype(qk_ref.dtype)
    for c in range(0, vt_ref.shape[1], COL_CHUNK):
        vt_ref[0, c:c + COL_CHUNK, :] = lax.dot_general(
            wvt_ref[c:c + COL_CHUNK, :], xn, (((1,), (1,)), ((), ())),
            preferred_element_type=F32).astype(vt_ref.dtype)


def _norm_qkv(h, g, w, b, s):
    t, d = h.shape
    per_seq = s // ROW_TILE
    w = w.astype(BF16)
    return pl.pallas_call(
        _norm_qkv_kernel,
        grid=(t // ROW_TILE,),
        in_specs=[pl.BlockSpec((ROW_TILE, d), lambda i: (i, 0)),
                  _const_spec((1, d)), _const_spec((d, 2 * d)), _const_spec((d, d))],
        out_specs=[pl.BlockSpec((ROW_TILE, 2 * d), lambda i: (i, 0)),
                   pl.BlockSpec((1, d, ROW_TILE), lambda i: (i // per_seq, 0, i % per_seq))],
        out_shape=[jax.ShapeDtypeStruct((t, 2 * d), BF16),
                   jax.ShapeDtypeStruct((b, d, s), BF16)],
        compiler_params=_params(1),
        name="norm_qkv",
    )(h, g.reshape(1, d), w[:, :2 * d], w[:, 2 * d:].T)


def _norm_glu_kernel(x_ref, g_ref, w_ref, b_ref, o_ref):
    xn = _rms(x_ref[...], g_ref[...]).astype(BF16)
    n = o_ref.shape[1]
    for c in range(0, n, COL_CHUNK):
        a = _dot(xn, w_ref[:, c:c + COL_CHUNK]) + b_ref[:, c:c + COL_CHUNK]
        gt = _dot(xn, w_ref[:, n + c:n + c + COL_CHUNK]) + b_ref[:, n + c:n + c + COL_CHUNK]
        o_ref[:, c:c + COL_CHUNK] = a * jax.nn.sigmoid(gt)


def _norm_glu(h, g, w, b):
    t, d = h.shape
    n = w.shape[1] // 2
    return pl.pallas_call(
        _norm_glu_kernel,
        grid=(t // ROW_TILE,),
        in_specs=[pl.BlockSpec((ROW_TILE, d), lambda i: (i, 0)),
                  _const_spec((1, d)), _const_spec((d, 2 * n)), _const_spec((1, 2 * n))],
        out_specs=pl.BlockSpec((ROW_TILE, n), lambda i: (i, 0)),
        out_shape=jax.ShapeDtypeStruct((t, n), F32),
        compiler_params=_params(1),
        name="norm_glu",
    )(h, g.reshape(1, d), w.astype(BF16), b.reshape(1, 2 * n))


def _attn_kernel(q_ref, k_ref, vt_ref, lq1_ref, lk1_ref, lq2_ref, lk2_ref, sg_ref, o_ref,
                 qq_ref, bias_ref, s0_ref, s1_ref, p0_ref, p1_ref, acc_ref, *, lambda_init):
    seq = k_ref.shape[1]
    tq, tk = ATTN_Q, ATTN_K
    n_q = seq // tq
    per_tile = tq // tk
    n_tasks = per_tile * n_q * (n_q + 1) // 2

    lane = lax.broadcasted_iota(jnp.int32, (tq, V_HEAD_DIM), 1)
    for i in range(n_q):
        q = q_ref[0, i * tq:(i + 1) * tq, :]
        zero = jnp.zeros_like(q)
        qq_ref[i, :tq, :] = jnp.where(lane < HEAD_DIM, q, zero)
        qq_ref[i, tq:, :] = jnp.where(lane >= HEAD_DIM, q, zero)

    key_pos = lax.broadcasted_iota(jnp.int32, (tk, 2 * tq), 0)
    query_chunk = (lax.broadcasted_iota(jnp.int32, (tk, 2 * tq), 1) % tq) // CHUNK
    bias_ref[0] = jnp.zeros((tk, 2 * tq), F32)
    for r in range(per_tile):
        bias_ref[r + 1] = jnp.where((key_pos + r * tk) // CHUNK <= query_chunk, 0.0, MASK_VALUE)

    ones = jnp.ones((SUM_ROWS, tk), BF16)
    lam = (jnp.exp(jnp.sum(lq1_ref[...] * lk1_ref[...], keepdims=True))
           - jnp.exp(jnp.sum(lq2_ref[...] * lk2_ref[...], keepdims=True)) + lambda_init)
    gain = sg_ref[...] * (1.0 - lambda_init)

    def scores(qi, ki, s_ref):
        qi = jnp.minimum(qi, n_q - 1)
        start = pl.multiple_of(jnp.minimum(ki, seq // tk - 1) * tk, tk)
        slot = jnp.clip(ki - qi * per_tile + 1, 0, per_tile)
        kblk = k_ref[0, pl.ds(start, tk), :]
        maxes = []
        for c in range(0, 2 * tq, MXU_COLS):
            s = lax.dot_general(kblk, qq_ref[qi, c:c + MXU_COLS, :], (((1,), (1,)), ((), ())),
                                preferred_element_type=F32)
            s = s + bias_ref[slot, :, c:c + MXU_COLS]
            s_ref[:, c:c + MXU_COLS] = s
            maxes.append(jnp.max(s.reshape(tk // SUBLANES, SUBLANES, MXU_COLS), axis=0))
        return jnp.concatenate(maxes, axis=1)

    def finalize(qi):
        r = 1.0 / acc_ref[V_HEAD_DIM:V_HEAD_DIM + 1, :]
        o = (acc_ref[:V_HEAD_DIM, :tq] * r[:, :tq]
             - acc_ref[:V_HEAD_DIM, tq:] * (lam * r[:, tq:]))
        o = o * lax.rsqrt(jnp.mean(o * o, axis=0, keepdims=True) + NORM_EPS) * gain
        o_ref[0, pl.ds(pl.multiple_of(qi * tq, tq), tq), :] = o.T.astype(o_ref.dtype)

    def sub_iteration(state, s_cur, s_nxt, p_cur, p_nxt):
        qi, ki, m, block_max, alpha_prev, ki_prev, qi_done = state
        n_blocks = (qi + 1) * per_tile
        last = ki == n_blocks - 1
        qi_next = jnp.where(last, qi + 1, qi)
        ki_next = jnp.where(last, 0, ki + 1)
        next_max = scores(qi_next, ki_next, s_nxt)
        m_old = jnp.where(ki == 0, MASK_VALUE, m)
        m_new = jnp.maximum(m_old, jnp.max(block_max, axis=0, keepdims=True))
        alpha = jnp.exp2(m_old - m_new)
        p_cur[...] = jnp.exp2(s_cur[...] - m_new).astype(BF16)
        start = pl.multiple_of(jnp.minimum(ki_prev, seq // tk - 1) * tk, tk)
        vt_ext = jnp.concatenate([vt_ref[0, :, pl.ds(start, tk)], ones], axis=0)
        for c in range(0, 2 * tq, MXU_COLS):
            cols = slice(c, c + MXU_COLS)
            acc_ref[:, cols] = (alpha_prev[:, cols] * acc_ref[:, cols]
                                + _dot(vt_ext, p_nxt[:, cols]))

        @pl.when(qi_done >= 0)
        def _():
            finalize(qi_done)

        done = jnp.where(last & (qi < n_q), qi, -1)
        return qi_next, ki_next, m_new, next_max, alpha, ki, done

    def body(_, state):
        state = sub_iteration(state, s0_ref, s1_ref, p0_ref, p1_ref)
        return sub_iteration(state, s1_ref, s0_ref, p1_ref, p0_ref)

    acc_ref[...] = jnp.zeros_like(acc_ref)
    p1_ref[...] = jnp.zeros_like(p1_ref)
    zero = jnp.int32(0)
    first_max = scores(zero, zero, s0_ref)
    state = (zero, zero, jnp.full((1, 2 * tq), MASK_VALUE, F32), first_max,
             jnp.ones((1, 2 * tq), F32), zero, jnp.int32(-1))
    lax.fori_loop(0, (n_tasks + 2) // 2, body, state)


def _attention(qk, vt, lq1, lk1, lq2, lk2, subln_g, lambda_init):
    b, s, _ = qk.shape
    d = N_HEADS * V_HEAD_DIM
    vec = lambda a: a.reshape(1, -1)
    head_rows = lambda col: pl.BlockSpec((1, s, V_HEAD_DIM), lambda bi, h: (bi, 0, col + h))
    return pl.pallas_call(
        functools.partial(_attn_kernel, lambda_init=lambda_init),
        grid=(b, N_HEADS),
        in_specs=[head_rows(0), head_rows(N_HEADS),
                  pl.BlockSpec((1, V_HEAD_DIM, s), lambda bi, h: (bi, h, 0)),
                  _const_spec((1, HEAD_DIM)), _const_spec((1, HEAD_DIM)),
                  _const_spec((1, HEAD_DIM)), _const_spec((1, HEAD_DIM)),
                  _const_spec((V_HEAD_DIM, 1))],
        out_specs=head_rows(0),
        out_shape=jax.ShapeDtypeStruct((b, s, d), BF16),
        scratch_shapes=[pltpu.VMEM((s // ATTN_Q, 2 * ATTN_Q, V_HEAD_DIM), BF16),
                        pltpu.VMEM((ATTN_Q // ATTN_K + 1, ATTN_K, 2 * ATTN_Q), F32),
                        pltpu.VMEM((ATTN_K, 2 * ATTN_Q), F32),
                        pltpu.VMEM((ATTN_K, 2 * ATTN_Q), F32),
                        pltpu.VMEM((ATTN_K, 2 * ATTN_Q), BF16),
                        pltpu.VMEM((ATTN_K, 2 * ATTN_Q), BF16),
                        pltpu.VMEM((V_HEAD_DIM + SUM_ROWS, 2 * ATTN_Q), F32)],
        compiler_params=_params(2),
        name="diff_attention",
    )(qk, qk, vt, vec(lq1), vec(lk1), vec(lq2), vec(lk2), subln_g.reshape(-1, 1))


def _proj_res_kernel(h_ref, a_ref, w_ref, o_ref):
    n = o_ref.shape[1]
    a = a_ref[...]
    for c in range(0, n, COL_CHUNK):
        o_ref[:, c:c + COL_CHUNK] = h_ref[:, c:c + COL_CHUNK] + _dot(a, w_ref[:, c:c + COL_CHUNK])


def _proj_res(h, a, w):
    t, d = h.shape
    return pl.pallas_call(
        _proj_res_kernel,
        grid=(t // ROW_TILE,),
        in_specs=[pl.BlockSpec((ROW_TILE, d), lambda i: (i, 0)),
                  pl.BlockSpec((ROW_TILE, d), lambda i: (i, 0)),
                  _const_spec((d, d))],
        out_specs=pl.BlockSpec((ROW_TILE, d), lambda i: (i, 0)),
        out_shape=jax.ShapeDtypeStruct((t, d), F32),
        compiler_params=_params(1),
        name="proj_res",
    )(h, a, w.astype(BF16))


def _ffn_kernel(h_ref, g_ref, wg_ref, wu_ref, wd_ref, o_ref):
    x = h_ref[...]
    xn = _rms(x, g_ref[...]).astype(BF16)
    o_ref[...] = x
    for c in range(0, wg_ref.shape[1], FF_CHUNK):
        gate = _dot(xn, wg_ref[:, c:c + FF_CHUNK])
        up = _dot(xn, wu_ref[:, c:c + FF_CHUNK])
        act = (gate * jax.nn.sigmoid(gate) * up).astype(BF16)
        o_ref[...] += _dot(act, wd_ref[c:c + FF_CHUNK, :])


def _ffn(h, g, w_gate, w_up, w_down):
    t, d = h.shape
    f = w_gate.shape[1]
    single = pl.Buffered(1)
    return pl.pallas_call(
        _ffn_kernel,
        grid=(t // ROW_TILE,),
        in_specs=[pl.BlockSpec((ROW_TILE, d), lambda i: (i, 0)),
                  _const_spec((1, d)),
                  pl.BlockSpec((d, f), lambda i: (0, 0), pipeline_mode=single),
                  pl.BlockSpec((d, f), lambda i: (0, 0), pipeline_mode=single),
                  pl.BlockSpec((f, d), lambda i: (0, 0), pipeline_mode=single)],
        out_specs=pl.BlockSpec((ROW_TILE, d), lambda i: (i, 0)),
        out_shape=jax.ShapeDtypeStruct((t, d), F32),
        compiler_params=_params(1),
        name="swiglu_ffn",
    )(h, g.reshape(1, d), w_gate.astype(BF16), w_up.astype(BF16), w_down.astype(BF16))


def _ple_kernel(h_ref, p_ref, g_ref, wgate_ref, wproj_ref, fg_ref, o_ref, *, final_norm):
    x = h_ref[...]
    xn = _rms(x, g_ref[...]).astype(BF16)
    gate = jax.nn.sigmoid(_dot(xn, wgate_ref[...]))
    y = x + _dot(p_ref[...].astype(BF16), wproj_ref[...]) * gate
    if final_norm:
        y = _rms(y, fg_ref[...])
    o_ref[...] = y


def _ple(h, p, g, w_gate, w_proj, final_g, final_norm):
    t, d = h.shape
    e = p.shape[1]
    return pl.pallas_call(
        functools.partial(_ple_kernel, final_norm=final_norm),
        grid=(t // ROW_TILE,),
        in_specs=[pl.BlockSpec((ROW_TILE, d), lambda i: (i, 0)),
                  pl.BlockSpec((ROW_TILE, e), lambda i: (i, 0)),
                  _const_spec((1, d)), _const_spec((d, d)), _const_spec((e, d)),
                  _const_spec((1, d))],
        out_specs=pl.BlockSpec((ROW_TILE, d), lambda i: (i, 0)),
        out_shape=jax.ShapeDtypeStruct((t, d), F32),
        compiler_params=_params(1),
        name="ple",
    )(h, p, g.reshape(1, d), w_gate.astype(BF16), w_proj.astype(BF16), final_g.reshape(1, d))


def _conv_kernel(u_ref, halo_ref, h_ref, dw_ref, dwb_ref, lng_ref, lnb_ref, w_ref, b_ref,
                 o_ref, ext_ref, y_ref):
    tile = u_ref.shape[1]
    first = pl.program_id(1) == 0
    ext_ref[0:CONV_HALO, :] = jnp.where(first, 0.0, halo_ref[0])
    ext_ref[CONV_HALO:, :] = u_ref[0]
    lead = CONV_HALO - (CONV_WIDTH - 1)
    for r in range(0, tile, CONV_ROWS):
        acc = jnp.zeros((CONV_ROWS, u_ref.shape[2]), F32) + dwb_ref[...]
        for k in range(CONV_WIDTH):
            acc = acc + ext_ref[r + lead + k:r + lead + k + CONV_ROWS, :] * dw_ref[k:k + 1, :]
        y_ref[r:r + CONV_ROWS, :] = acc
    y = y_ref[...]
    mu = jnp.mean(y, axis=-1, keepdims=True)
    yc = y - mu
    y = yc * lax.rsqrt(jnp.mean(yc * yc, axis=-1, keepdims=True) + NORM_EPS) * lng_ref[...] + lnb_ref[...]
    y = (y * jax.nn.sigmoid(y)).astype(BF16)
    o_ref[0] = h_ref[0] + _dot(y, w_ref[...]) + b_ref[...]


def _conv_tail(u, h, dw_w, dw_b, ln_g, ln_b, pw2_w, pw2_b):
    b, s, d = u.shape
    per_tile = CONV_TILE // CONV_HALO
    vec = lambda a: a.reshape(1, d)
    return pl.pallas_call(
        _conv_kernel,
        grid=(b, s // CONV_TILE),
        in_specs=[pl.BlockSpec((1, CONV_TILE, d), lambda bi, i: (bi, i, 0)),
                  pl.BlockSpec((1, CONV_HALO, d),
                               lambda bi, i: (bi, jnp.maximum(i * per_tile - 1, 0), 0)),
                  pl.BlockSpec((1, CONV_TILE, d), lambda bi, i: (bi, i, 0)),
                  _const_spec((CONV_WIDTH, d)), _const_spec((1, d)), _const_spec((1, d)),
                  _const_spec((1, d)), _const_spec((d, d)), _const_spec((1, d))],
        out_specs=pl.BlockSpec((1, CONV_TILE, d), lambda bi, i: (bi, i, 0)),
        out_shape=jax.ShapeDtypeStruct((b, s, d), F32),
        scratch_shapes=[pltpu.VMEM((CONV_TILE + CONV_HALO, d), F32),
                        pltpu.VMEM((CONV_TILE, d), F32)],
        compiler_params=_params(2),
        name="conv_tail",
    )(u, u, h, dw_w, vec(dw_b), vec(ln_g), vec(ln_b), pw2_w.astype(BF16), vec(pw2_b))


def _lambda_init(layer_idx):
    return 0.8 - 0.6 * math.exp(-0.3 * layer_idx)


def kernel(x, p, attn_norm_g, w_qkv, lambda_q1, lambda_k1, lambda_q2, lambda_k2, subln_g, w_o, conv_norm_g, conv_pw1_w, conv_pw1_b, conv_dw_w, conv_dw_b, conv_ln_g, conv_ln_b, conv_pw2_w, conv_pw2_b, ffn_norm_g, w_gate, w_up, w_down, ple_norm_g, w_ple_gate, w_ple_proj, final_norm_g):
    b, s, d = x.shape
    depth = p.shape[0]
    t = b * s
    assert d == N_HEADS * V_HEAD_DIM and t % ROW_TILE == 0
    assert s % ATTN_Q == 0 and ATTN_Q % ATTN_K == 0 and ATTN_K % CHUNK == 0
    assert s % ROW_TILE == 0 and s % CONV_TILE == 0
    h = x.reshape(t, d)
    for i in range(depth):
        j = i // 2
        if i % 2 == 0:
            qk, vt = _norm_qkv(h, attn_norm_g[j], w_qkv[j], b, s)
            o = _attention(qk.reshape(b, s, 2 * d), vt, lambda_q1[j], lambda_k1[j], lambda_q2[j],
                           lambda_k2[j], subln_g[j], _lambda_init(i))
            h = _proj_res(h, o.reshape(t, d), w_o[j])
        else:
            u = _norm_glu(h, conv_norm_g[j], conv_pw1_w[j], conv_pw1_b[j])
            h = _conv_tail(u.reshape(b, s, d), h.reshape(b, s, d), conv_dw_w[j], conv_dw_b[j],
                           conv_ln_g[j], conv_ln_b[j], conv_pw2_w[j], conv_pw2_b[j]).reshape(t, d)
        h = _ffn(h, ffn_norm_g[i], w_gate[i], w_up[i], w_down[i])
        h = _ple(h, p[i].reshape(t, -1), ple_norm_g[i], w_ple_gate[i], w_ple_proj[i],
                 final_norm_g, final_norm=(i == depth - 1))
    return h.reshape(b, s, d)
```

```python
import functools
import math

import jax
import jax.numpy as jnp
from jax import lax
from jax.experimental import pallas as pl
from jax.experimental.pallas import tpu as pltpu

F32 = jnp.float32
BF16 = jnp.bfloat16

NORM_EPS = 1e-6
CHUNK = 64
N_HEADS = 8
HEAD_DIM = 64
V_HEAD_DIM = 2 * HEAD_DIM
CONV_WIDTH = 31
CONV_HALO = 32
MASK_VALUE = -1e30

SCORE_SCALE = HEAD_DIM ** -0.5 * math.log2(math.e)
SUM_ROWS = 16
SUBLANES = 8
MXU_COLS = 256
SOFTMAX_ROWS = 32

ROW_TILE = 512
ATTN_Q = 512
ATTN_K = 512
ATTN_UNROLL = 2
CONV_TILE = 256
CONV_ROWS = 32
CONV_COLS = 512
COL_CHUNK = 512
FF_CHUNK = 256
VMEM_LIMIT = 56 * 1024 * 1024


def _params(n_axes, flags=None):
    return pltpu.CompilerParams(dimension_semantics=("arbitrary",) * n_axes,
                                vmem_limit_bytes=VMEM_LIMIT, flags=flags)


def _rms(x, g):
    return x * lax.rsqrt(jnp.mean(x * x, axis=-1, keepdims=True) + NORM_EPS) * g


def _dot(a, b):
    return jnp.dot(a, b, preferred_element_type=F32)


def _const_spec(shape):
    return pl.BlockSpec(shape, lambda *_: (0,) * len(shape))


def _norm_qkv_kernel(x_ref, g_ref, wk_ref, wqvt_ref, k_ref, qvt_ref):
    xn = _rms(x_ref[...], g_ref[...]).astype(BF16)
    d = k_ref.shape[1]
    for c in range(0, d, COL_CHUNK):
        k_ref[:, c:c + COL_CHUNK] = _dot(xn, wk_ref[:, c:c + COL_CHUNK]).astype(k_ref.dtype)
    for c in range(0, 2 * d, COL_CHUNK):
        y = lax.dot_general(wqvt_ref[c:c + COL_CHUNK, :], xn, (((1,), (1,)), ((), ())),
                            preferred_element_type=F32)
        if c < d:
            y = y * SCORE_SCALE
        qvt_ref[0, c:c + COL_CHUNK, :] = y.astype(qvt_ref.dtype)


def _norm_qkv(h, g, w, b, s):
    t, d = h.shape
    per_seq = s // ROW_TILE
    w = w.astype(BF16)
    wqvt = jnp.concatenate([w[:, :d], w[:, 2 * d:]], axis=1).T
    return pl.pallas_call(
        _norm_qkv_kernel,
        grid=(t // ROW_TILE,),
        in_specs=[pl.BlockSpec((ROW_TILE, d), lambda i: (i, 0)),
                  _const_spec((1, d)), _const_spec((d, d)), _const_spec((2 * d, d))],
        out_specs=[pl.BlockSpec((ROW_TILE, d), lambda i: (i, 0)),
                   pl.BlockSpec((1, 2 * d, ROW_TILE), lambda i: (i // per_seq, 0, i % per_seq))],
        out_shape=[jax.ShapeDtypeStruct((t, d), BF16),
                   jax.ShapeDtypeStruct((b, 2 * d, s), BF16)],
        compiler_params=_params(1),
        name="norm_qkv",
    )(h, g.reshape(1, d), w[:, d:2 * d], wqvt)


def _norm_glu_kernel(x_ref, g_ref, w_ref, b_ref, o_ref):
    xn = _rms(x_ref[...], g_ref[...]).astype(BF16)
    n = o_ref.shape[1]
    for c in range(0, n, COL_CHUNK):
        a = _dot(xn, w_ref[:, c:c + COL_CHUNK]) + b_ref[:, c:c + COL_CHUNK]
        gt = _dot(xn, w_ref[:, n + c:n + c + COL_CHUNK]) + b_ref[:, n + c:n + c + COL_CHUNK]
        o_ref[:, c:c + COL_CHUNK] = a * jax.nn.sigmoid(gt)


def _norm_glu(h, g, w, b):
    t, d = h.shape
    n = w.shape[1] // 2
    return pl.pallas_call(
        _norm_glu_kernel,
        grid=(t // ROW_TILE,),
        in_specs=[pl.BlockSpec((ROW_TILE, d), lambda i: (i, 0)),
                  _const_spec((1, d)), _const_spec((d, 2 * n)), _const_spec((1, 2 * n))],
        out_specs=pl.BlockSpec((ROW_TILE, n), lambda i: (i, 0)),
        out_shape=jax.ShapeDtypeStruct((t, n), F32),
        compiler_params=_params(1),
        name="norm_glu",
    )(h, g.reshape(1, d), w.astype(BF16), b.reshape(1, 2 * n))


def _attn_kernel(qt_ref, k_ref, vt_ref, lq1_ref, lk1_ref, lq2_ref, lk2_ref, sg_ref, o_ref,
                 qq_ref, kx_ref, qx_ref, s0_ref, s1_ref, p0_ref, p1_ref, acc_ref, *, lambda_init):
    seq = k_ref.shape[1]
    tq, tk = ATTN_Q, ATTN_K
    n_q = seq // tq
    per_tile = tq // tk
    n_tasks = per_tile * n_q * (n_q + 1) // 2

    feature = lax.broadcasted_iota(jnp.int32, (V_HEAD_DIM, tq), 0)
    for i in range(n_q):
        q = qt_ref[0, :, i * tq:(i + 1) * tq]
        zero = jnp.zeros_like(q)
        qq_ref[i, :, :tq] = jnp.where(feature < HEAD_DIM, q, zero)
        qq_ref[i, :, tq:] = jnp.where(feature >= HEAD_DIM, q, zero)

    chunks = tk // CHUNK
    feat = lax.broadcasted_iota(jnp.int32, (tk, V_HEAD_DIM), 1)
    key_chunk = lax.broadcasted_iota(jnp.int32, (tk, V_HEAD_DIM), 0) // CHUNK
    kx_ref[...] = jnp.where(feat == key_chunk, 1.0, 0.0).astype(BF16)
    feat = lax.broadcasted_iota(jnp.int32, (V_HEAD_DIM, 2 * tq), 0)
    query_chunk = (lax.broadcasted_iota(jnp.int32, (V_HEAD_DIM, 2 * tq), 1) % tq) // CHUNK
    qx_ref[0] = jnp.zeros((V_HEAD_DIM, 2 * tq), BF16)
    for r in range(per_tile):
        hidden = (feat < chunks) & (feat + r * chunks > query_chunk)
        qx_ref[r + 1] = jnp.where(hidden, MASK_VALUE, 0.0).astype(BF16)

    ones = jnp.ones((SUM_ROWS, tk), BF16)
    lam = (jnp.exp(jnp.sum(lq1_ref[...] * lk1_ref[...], keepdims=True))
           - jnp.exp(jnp.sum(lq2_ref[...] * lk2_ref[...], keepdims=True)) + lambda_init)
    gain = sg_ref[...] * (1.0 - lambda_init)

    def scores(qi, ki, s_ref):
        qi = jnp.minimum(qi, n_q - 1)
        start = pl.multiple_of(jnp.minimum(ki, seq // tk - 1) * tk, tk)
        slot = jnp.clip(ki - qi * per_tile + 1, 0, per_tile)
        k_ext = jnp.concatenate([k_ref[0, pl.ds(start, tk), :], kx_ref[...]], axis=1)
        maxes = []
        for c in range(0, 2 * tq, MXU_COLS):
            q_ext = jnp.concatenate([qq_ref[qi, :, c:c + MXU_COLS],
                                     qx_ref[slot, :, c:c + MXU_COLS]], axis=0)
            s = _dot(k_ext, q_ext)
            s_ref[:, c:c + MXU_COLS] = s
            maxes.append(jnp.max(s.reshape(tk // SUBLANES, SUBLANES, MXU_COLS), axis=0))
        return jnp.concatenate(maxes, axis=1)

    def finalize(qi):
        buf = lax.rem(qi, 2)
        r = 1.0 / acc_ref[buf, V_HEAD_DIM:V_HEAD_DIM + 1, :]
        o = (acc_ref[buf, :V_HEAD_DIM, :tq] * r[:, :tq]
             - acc_ref[buf, :V_HEAD_DIM, tq:] * (lam * r[:, tq:]))
        o = o * lax.rsqrt(jnp.mean(o * o, axis=0, keepdims=True) + NORM_EPS) * gain
        o_ref[0, pl.ds(pl.multiple_of(qi * tq, tq), tq), :] = o.T.astype(o_ref.dtype)

    def sub_iteration(state, s_cur, s_nxt, p_cur, p_nxt):
        qi, ki, m, block_max, alpha_prev, qi_prev, ki_prev, _ = state
        n_blocks = (qi + 1) * per_tile
        last = ki == n_blocks - 1
        qi_next = jnp.where(last, qi + 1, qi)
        ki_next = jnp.where(last, 0, ki + 1)
        next_max = scores(qi_next, ki_next, s_nxt)
        m_old = jnp.where(ki == 0, MASK_VALUE, m)
        m_new = jnp.maximum(m_old, jnp.max(block_max, axis=0, keepdims=True))
        alpha = jnp.exp2(m_old - m_new)
        shifts = [jnp.broadcast_to(m_new, (SUBLANES, 2 * tq))] * 2
        for g in range(tk // SOFTMAX_ROWS):
            rows = slice(g * SOFTMAX_ROWS, (g + 1) * SOFTMAX_ROWS)
            x = s_cur[rows, :].reshape(SOFTMAX_ROWS // SUBLANES, SUBLANES, 2 * tq)
            e = jnp.exp2(x - shifts[g % 2]).reshape(SOFTMAX_ROWS, 2 * tq)
            p_cur[rows, :] = e.astype(BF16)
            shifts[g % 2] = m_new + 0.0 * e[SOFTMAX_ROWS - SUBLANES:, :]
        start = pl.multiple_of(jnp.minimum(ki_prev, seq // tk - 1) * tk, tk)
        vt_ext = jnp.concatenate([vt_ref[0, :, pl.ds(start, tk)], ones], axis=0)
        buf = lax.rem(qi_prev, 2)
        for c in range(0, 2 * tq, MXU_COLS):
            cols = slice(c, c + MXU_COLS)
            acc_ref[buf, :, cols] = (alpha_prev[:, cols] * acc_ref[buf, :, cols]
                                     + _dot(vt_ext, p_nxt[:, cols]))
        done = jnp.where(last & (qi < n_q), qi, -1)
        return qi_next, ki_next, m_new, next_max, alpha, qi, ki, done

    def body(_, state):
        completed = []
        for _ in range(ATTN_UNROLL // 2):
            completed.append(state[-1])
            state = sub_iteration(state, s0_ref, s1_ref, p0_ref, p1_ref)
            completed.append(state[-1])
            state = sub_iteration(state, s1_ref, s0_ref, p1_ref, p0_ref)
        for done in completed:
            pl.when(done >= 0)(functools.partial(finalize, done))
        return state

    acc_ref[...] = jnp.zeros_like(acc_ref)
    p1_ref[...] = jnp.zeros_like(p1_ref)
    zero = jnp.int32(0)
    first_max = scores(zero, zero, s0_ref)
    state = (zero, zero, jnp.full((1, 2 * tq), MASK_VALUE, F32), first_max,
             jnp.ones((1, 2 * tq), F32), zero, zero, jnp.int32(-1))
    lax.fori_loop(0, pl.cdiv(n_tasks + 1, ATTN_UNROLL), body, state)


def _attention(k, qvt, lq1, lk1, lq2, lk2, subln_g, lambda_init):
    b, s, d = k.shape
    vec = lambda a: a.reshape(1, -1)
    head_rows = pl.BlockSpec((1, s, V_HEAD_DIM), lambda bi, h: (bi, 0, h))
    head_cols = lambda row: pl.BlockSpec((1, V_HEAD_DIM, s), lambda bi, h: (bi, row + h, 0))
    return pl.pallas_call(
        functools.partial(_attn_kernel, lambda_init=lambda_init),
        grid=(b, N_HEADS),
        in_specs=[head_cols(0), head_rows, head_cols(N_HEADS),
                  _const_spec((1, HEAD_DIM)), _const_spec((1, HEAD_DIM)),
                  _const_spec((1, HEAD_DIM)), _const_spec((1, HEAD_DIM)),
                  _const_spec((V_HEAD_DIM, 1))],
        out_specs=head_rows,
        out_shape=jax.ShapeDtypeStruct((b, s, d), BF16),
        scratch_shapes=[pltpu.VMEM((s // ATTN_Q, V_HEAD_DIM, 2 * ATTN_Q), BF16),
                        pltpu.VMEM((ATTN_K, V_HEAD_DIM), BF16),
                        pltpu.VMEM((ATTN_Q // ATTN_K + 1, V_HEAD_DIM, 2 * ATTN_Q), BF16),
                        pltpu.VMEM((ATTN_K, 2 * ATTN_Q), F32),
                        pltpu.VMEM((ATTN_K, 2 * ATTN_Q), F32),
                        pltpu.VMEM((ATTN_K, 2 * ATTN_Q), BF16),
                        pltpu.VMEM((ATTN_K, 2 * ATTN_Q), BF16),
                        pltpu.VMEM((2, V_HEAD_DIM + SUM_ROWS, 2 * ATTN_Q), F32)],
        compiler_params=_params(2),
        name="diff_attention",
    )(qvt, k, qvt, vec(lq1), vec(lk1), vec(lq2), vec(lk2), subln_g.reshape(-1, 1))


def _proj_res_kernel(h_ref, a_ref, w_ref, o_ref):
    n = o_ref.shape[1]
    a = a_ref[...]
    for c in range(0, n, COL_CHUNK):
        o_ref[:, c:c + COL_CHUNK] = h_ref[:, c:c + COL_CHUNK] + _dot(a, w_ref[:, c:c + COL_CHUNK])


def _proj_res(h, a, w):
    t, d = h.shape
    return pl.pallas_call(
        _proj_res_kernel,
        grid=(t // ROW_TILE,),
        in_specs=[pl.BlockSpec((ROW_TILE, d), lambda i: (i, 0)),
                  pl.BlockSpec((ROW_TILE, d), lambda i: (i, 0)),
                  _const_spec((d, d))],
        out_specs=pl.BlockSpec((ROW_TILE, d), lambda i: (i, 0)),
        out_shape=jax.ShapeDtypeStruct((t, d), F32),
        compiler_params=_params(1),
        name="proj_res",
    )(h, a, w.astype(BF16))


def _ffn_kernel(h_ref, g_ref, wg_ref, wu_ref, wd_ref, o_ref):
    x = h_ref[...]
    xn = _rms(x, g_ref[...]).astype(BF16)
    o_ref[...] = x
    for c in range(0, wg_ref.shape[1], FF_CHUNK):
        gate = _dot(xn, wg_ref[:, c:c + FF_CHUNK])
        up = _dot(xn, wu_ref[:, c:c + FF_CHUNK])
        act = (gate * jax.nn.sigmoid(gate) * up).astype(BF16)
        o_ref[...] += _dot(act, wd_ref[c:c + FF_CHUNK, :])


def _ffn(h, g, w_gate, w_up, w_down):
    t, d = h.shape
    f = w_gate.shape[1]
    single = pl.Buffered(1)
    return pl.pallas_call(
        _ffn_kernel,
        grid=(t // ROW_TILE,),
        in_specs=[pl.BlockSpec((ROW_TILE, d), lambda i: (i, 0)),
                  _const_spec((1, d)),
                  pl.BlockSpec((d, f), lambda i: (0, 0), pipeline_mode=single),
                  pl.BlockSpec((d, f), lambda i: (0, 0), pipeline_mode=single),
                  pl.BlockSpec((f, d), lambda i: (0, 0), pipeline_mode=single)],
        out_specs=pl.BlockSpec((ROW_TILE, d), lambda i: (i, 0)),
        out_shape=jax.ShapeDtypeStruct((t, d), F32),
        compiler_params=_params(1),
        name="swiglu_ffn",
    )(h, g.reshape(1, d), w_gate.astype(BF16), w_up.astype(BF16), w_down.astype(BF16))


def _ple_kernel(h_ref, p_ref, g_ref, wgate_ref, wproj_ref, fg_ref, o_ref, *, final_norm):
    x = h_ref[...]
    xn = _rms(x, g_ref[...]).astype(BF16)
    gate = jax.nn.sigmoid(_dot(xn, wgate_ref[...]))
    y = x + _dot(p_ref[...].astype(BF16), wproj_ref[...]) * gate
    if final_norm:
        y = _rms(y, fg_ref[...])
    o_ref[...] = y


def _ple(h, p, g, w_gate, w_proj, final_g, final_norm):
    t, d = h.shape
    e = p.shape[1]
    return pl.pallas_call(
        functools.partial(_ple_kernel, final_norm=final_norm),
        grid=(t // ROW_TILE,),
        in_specs=[pl.BlockSpec((ROW_TILE, d), lambda i: (i, 0)),
                  pl.BlockSpec((ROW_TILE, e), lambda i: (i, 0)),
                  _const_spec((1, d)), _const_spec((d, d)), _const_spec((e, d)),
                  _const_spec((1, d))],
        out_specs=pl.BlockSpec((ROW_TILE, d), lambda i: (i, 0)),
        out_shape=jax.ShapeDtypeStruct((t, d), F32),
        compiler_params=_params(1),
        name="ple",
    )(h, p, g.reshape(1, d), w_gate.astype(BF16), w_proj.astype(BF16), final_g.reshape(1, d))


def _conv_kernel(u_ref, halo_ref, h_ref, dw_ref, dwb_ref, lng_ref, lnb_ref, w_ref, b_ref,
                 o_ref, ext_ref, shift_ref, y_ref):
    tile, d = u_ref.shape[1], u_ref.shape[2]
    first = pl.program_id(1) == 0
    ext_ref[0:CONV_HALO, :] = jnp.where(first, 0.0, halo_ref[0])
    ext_ref[CONV_HALO:, :] = u_ref[0]
    span = shift_ref.shape[1]
    for r in range(1, SUBLANES):
        shift_ref[r - 1] = ext_ref[r:r + span, :]
    lead = CONV_HALO - (CONV_WIDTH - 1)
    groups = CONV_ROWS // SUBLANES
    for c in range(0, d, CONV_COLS):
        cols = slice(c, c + CONV_COLS)
        for r0 in range(0, tile, CONV_ROWS):
            acc = jnp.zeros((groups, SUBLANES, CONV_COLS), F32) + dwb_ref[:, cols]
            for k in range(CONV_WIDTH):
                r = (lead + k) % SUBLANES
                a = r0 + lead + k - r
                x = ext_ref[a:a + CONV_ROWS, cols] if r == 0 else shift_ref[r - 1, a:a + CONV_ROWS, cols]
                acc = acc + x.reshape(groups, SUBLANES, CONV_COLS) * dw_ref[k, :, cols]
            y_ref[r0:r0 + CONV_ROWS, cols] = acc.reshape(CONV_ROWS, CONV_COLS)
    y = y_ref[...]
    mu = jnp.mean(y, axis=-1, keepdims=True)
    yc = y - mu
    y = yc * lax.rsqrt(jnp.mean(yc * yc, axis=-1, keepdims=True) + NORM_EPS) * lng_ref[...] + lnb_ref[...]
    y = (y * jax.nn.sigmoid(y)).astype(BF16)
    o_ref[0] = h_ref[0] + _dot(y, w_ref[...]) + b_ref[...]


def _conv_tail(u, h, dw_w, dw_b, ln_g, ln_b, pw2_w, pw2_b):
    b, s, d = u.shape
    per_tile = CONV_TILE // CONV_HALO
    vec = lambda a: a.reshape(1, d)
    return pl.pallas_call(
        _conv_kernel,
        grid=(b, s // CONV_TILE),
        in_specs=[pl.BlockSpec((1, CONV_TILE, d), lambda bi, i: (bi, i, 0)),
                  pl.BlockSpec((1, CONV_HALO, d),
                               lambda bi, i: (bi, jnp.maximum(i * per_tile - 1, 0), 0)),
                  pl.BlockSpec((1, CONV_TILE, d), lambda bi, i: (bi, i, 0)),
                  _const_spec((CONV_WIDTH, SUBLANES, d)), _const_spec((1, d)), _const_spec((1, d)),
                  _const_spec((1, d)), _const_spec((d, d)), _const_spec((1, d))],
        out_specs=pl.BlockSpec((1, CONV_TILE, d), lambda bi, i: (bi, i, 0)),
        out_shape=jax.ShapeDtypeStruct((b, s, d), F32),
        scratch_shapes=[pltpu.VMEM((CONV_TILE + CONV_HALO, d), F32),
                        pltpu.VMEM((SUBLANES - 1, CONV_TILE + CONV_HALO - SUBLANES, d), F32),
                        pltpu.VMEM((CONV_TILE, d), F32)],
        compiler_params=_params(2),
        name="conv_tail",
    )(u, u, h, jnp.broadcast_to(dw_w[:, None, :], (CONV_WIDTH, SUBLANES, d)), vec(dw_b),
      vec(ln_g), vec(ln_b), pw2_w.astype(BF16), vec(pw2_b))


def _lambda_init(layer_idx):
    return 0.8 - 0.6 * math.exp(-0.3 * layer_idx)


def kernel(x, p, attn_norm_g, w_qkv, lambda_q1, lambda_k1, lambda_q2, lambda_k2, subln_g, w_o, conv_norm_g, conv_pw1_w, conv_pw1_b, conv_dw_w, conv_dw_b, conv_ln_g, conv_ln_b, conv_pw2_w, conv_pw2_b, ffn_norm_g, w_gate, w_up, w_down, ple_norm_g, w_ple_gate, w_ple_proj, final_norm_g):
    b, s, d = x.shape
    depth = p.shape[0]
    t = b * s
    assert d == N_HEADS * V_HEAD_DIM and t % ROW_TILE == 0
    assert s % ATTN_Q == 0 and ATTN_Q % ATTN_K == 0 and ATTN_K % CHUNK == 0
    assert s % ROW_TILE == 0 and s % CONV_TILE == 0
    h = x.reshape(t, d)
    for i in range(depth):
        j = i // 2
        if i % 2 == 0:
            k, qvt = _norm_qkv(h, attn_norm_g[j], w_qkv[j], b, s)
            o = _attention(k.reshape(b, s, d), qvt, lambda_q1[j], lambda_k1[j], lambda_q2[j],
                           lambda_k2[j], subln_g[j], _lambda_init(i))
            h = _proj_res(h, o.reshape(t, d), w_o[j])
        else:
            u = _norm_glu(h, conv_norm_g[j], conv_pw1_w[j], conv_pw1_b[j])
            h = _conv_tail(u.reshape(b, s, d), h.reshape(b, s, d), conv_dw_w[j], conv_dw_b[j],
                           conv_ln_g[j], conv_ln_b[j], conv_pw2_w[j], conv_pw2_b[j]).reshape(t, d)
        h = _ffn(h, ffn_norm_g[i], w_gate[i], w_up[i], w_down[i])
        h = _ple(h, p[i].reshape(t, -1), ple_norm_g[i], w_ple_gate[i], w_ple_proj[i],
                 final_norm_g, final_norm=(i == depth - 1))
    return h.reshape(b, s, d)
```

```python
import functools
import math

import jax
import jax.numpy as jnp
from jax import lax
from jax.experimental import pallas as pl
from jax.experimental.pallas import tpu as pltpu

F32 = jnp.float32
BF16 = jnp.bfloat16

NORM_EPS = 1e-6
CHUNK = 64
N_HEADS = 8
HEAD_DIM = 64
V_HEAD_DIM = 2 * HEAD_DIM
CONV_WIDTH = 31
CONV_HALO = 32
MASK_VALUE = -1e30

SCORE_SCALE = HEAD_DIM ** -0.5 * math.log2(math.e)
SUM_ROWS = 16
SUBLANES = 8
MXU_COLS = 256
SOFTMAX_ROWS = 64
SOFTMAX_CHAINS = 2

ROW_TILE = 512
ATTN_Q = 512
ATTN_K = 512
ATTN_UNROLL = 2
CONV_TILE = 256
CONV_ROWS = 32
CONV_COLS = 512
COL_CHUNK = 512
FF_CHUNK = 256
VMEM_LIMIT = 56 * 1024 * 1024


def _params(n_axes, flags=None):
    return pltpu.CompilerParams(dimension_semantics=("arbitrary",) * n_axes,
                                vmem_limit_bytes=VMEM_LIMIT, flags=flags)


def _rms(x, g):
    return x * lax.rsqrt(jnp.mean(x * x, axis=-1, keepdims=True) + NORM_EPS) * g


def _dot(a, b):
    return jnp.dot(a, b, preferred_element_type=F32)


def _const_spec(shape):
    return pl.BlockSpec(shape, lambda *_: (0,) * len(shape))


def _norm_qkv_kernel(x_ref, g_ref, wk_ref, wqvt_ref, k_ref, qvt_ref):
    xn = _rms(x_ref[...], g_ref[...]).astype(BF16)
    d = k_ref.shape[1]
    for c in range(0, d, COL_CHUNK):
        k_ref[:, c:c + COL_CHUNK] = _dot(xn, wk_ref[:, c:c + COL_CHUNK]).astype(k_ref.dtype)
    for c in range(0, 2 * d, COL_CHUNK):
        y = lax.dot_general(wqvt_ref[c:c + COL_CHUNK, :], xn, (((1,), (1,)), ((), ())),
                            preferred_element_type=F32)
        if c < d:
            y = y * SCORE_SCALE
        qvt_ref[0, c:c + COL_CHUNK, :] = y.astype(qvt_ref.dtype)


def _norm_qkv(h, g, w, b, s):
    t, d = h.shape
    per_seq = s // ROW_TILE
    w = w.astype(BF16)
    wqvt = jnp.concatenate([w[:, :d], w[:, 2 * d:]], axis=1).T
    return pl.pallas_call(
        _norm_qkv_kernel,
        grid=(t // ROW_TILE,),
        in_specs=[pl.BlockSpec((ROW_TILE, d), lambda i: (i, 0)),
                  _const_spec((1, d)), _const_spec((d, d)), _const_spec((2 * d, d))],
        out_specs=[pl.BlockSpec((ROW_TILE, d), lambda i: (i, 0)),
                   pl.BlockSpec((1, 2 * d, ROW_TILE), lambda i: (i // per_seq, 0, i % per_seq))],
        out_shape=[jax.ShapeDtypeStruct((t, d), BF16),
                   jax.ShapeDtypeStruct((b, 2 * d, s), BF16)],
        compiler_params=_params(1),
        name="norm_qkv",
    )(h, g.reshape(1, d), w[:, d:2 * d], wqvt)


def _norm_glu_kernel(x_ref, g_ref, w_ref, b_ref, o_ref):
    xn = _rms(x_ref[...], g_ref[...]).astype(BF16)
    n = o_ref.shape[1]
    for c in range(0, n, COL_CHUNK):
        a = _dot(xn, w_ref[:, c:c + COL_CHUNK]) + b_ref[:, c:c + COL_CHUNK]
        gt = _dot(xn, w_ref[:, n + c:n + c + COL_CHUNK]) + b_ref[:, n + c:n + c + COL_CHUNK]
        o_ref[:, c:c + COL_CHUNK] = a * jax.nn.sigmoid(gt)


def _norm_glu(h, g, w, b):
    t, d = h.shape
    n = w.shape[1] // 2
    return pl.pallas_call(
        _norm_glu_kernel,
        grid=(t // ROW_TILE,),
        in_specs=[pl.BlockSpec((ROW_TILE, d), lambda i: (i, 0)),
                  _const_spec((1, d)), _const_spec((d, 2 * n)), _const_spec((1, 2 * n))],
        out_specs=pl.BlockSpec((ROW_TILE, n), lambda i: (i, 0)),
        out_shape=jax.ShapeDtypeStruct((t, n), F32),
        compiler_params=_params(1),
        name="norm_glu",
    )(h, g.reshape(1, d), w.astype(BF16), b.reshape(1, 2 * n))


def _attn_kernel(qt_ref, k_ref, vt_ref, lq1_ref, lk1_ref, lq2_ref, lk2_ref, sg_ref, o_ref,
                 qq_ref, kx_ref, qx_ref, s0_ref, s1_ref, p0_ref, p1_ref, acc_ref, *, lambda_init):
    seq = k_ref.shape[1]
    tq, tk = ATTN_Q, ATTN_K
    n_q = seq // tq
    per_tile = tq // tk
    n_tasks = per_tile * n_q * (n_q + 1) // 2

    feature = lax.broadcasted_iota(jnp.int32, (V_HEAD_DIM, tq), 0)
    for i in range(n_q):
        q = qt_ref[0, :, i * tq:(i + 1) * tq]
        zero = jnp.zeros_like(q)
        qq_ref[i, :, :tq] = jnp.where(feature < HEAD_DIM, q, zero)
        qq_ref[i, :, tq:] = jnp.where(feature >= HEAD_DIM, q, zero)

    chunks = tk // CHUNK
    feat = lax.broadcasted_iota(jnp.int32, (tk, V_HEAD_DIM), 1)
    key_chunk = lax.broadcasted_iota(jnp.int32, (tk, V_HEAD_DIM), 0) // CHUNK
    kx_ref[...] = jnp.where(feat == key_chunk, 1.0, 0.0).astype(BF16)
    feat = lax.broadcasted_iota(jnp.int32, (V_HEAD_DIM, 2 * tq), 0)
    query_chunk = (lax.broadcasted_iota(jnp.int32, (V_HEAD_DIM, 2 * tq), 1) % tq) // CHUNK
    qx_ref[0] = jnp.zeros((V_HEAD_DIM, 2 * tq), BF16)
    for r in range(per_tile):
        hidden = (feat < chunks) & (feat + r * chunks > query_chunk)
        qx_ref[r + 1] = jnp.where(hidden, MASK_VALUE, 0.0).astype(BF16)

    ones = jnp.ones((SUM_ROWS, tk), BF16)
    lam = (jnp.exp(jnp.sum(lq1_ref[...] * lk1_ref[...], keepdims=True))
           - jnp.exp(jnp.sum(lq2_ref[...] * lk2_ref[...], keepdims=True)) + lambda_init)
    gain = sg_ref[...] * (1.0 - lambda_init)

    def scores(qi, ki, s_ref):
        qi = jnp.minimum(qi, n_q - 1)
        start = pl.multiple_of(jnp.minimum(ki, seq // tk - 1) * tk, tk)
        slot = jnp.clip(ki - qi * per_tile + 1, 0, per_tile)
        k_ext = jnp.concatenate([k_ref[0, pl.ds(start, tk), :], kx_ref[...]], axis=1)
        maxes = []
        for c in range(0, 2 * tq, MXU_COLS):
            q_ext = jnp.concatenate([qq_ref[qi, :, c:c + MXU_COLS],
                                     qx_ref[slot, :, c:c + MXU_COLS]], axis=0)
            s = _dot(k_ext, q_ext)
            s_ref[:, c:c + MXU_COLS] = s
            maxes.append(jnp.max(s.reshape(tk // SUBLANES, SUBLANES, MXU_COLS), axis=0))
        return jnp.concatenate(maxes, axis=1)

    def finalize(qi):
        buf = lax.rem(qi, 2)
        r = 1.0 / acc_ref[buf, V_HEAD_DIM:V_HEAD_DIM + 1, :]
        o = (acc_ref[buf, :V_HEAD_DIM, :tq] * r[:, :tq]
             - acc_ref[buf, :V_HEAD_DIM, tq:] * (lam * r[:, tq:]))
        o = o * lax.rsqrt(jnp.mean(o * o, axis=0, keepdims=True) + NORM_EPS) * gain
        o_ref[0, pl.ds(pl.multiple_of(qi * tq, tq), tq), :] = o.T.astype(o_ref.dtype)

    def sub_iteration(state, s_cur, s_nxt, p_cur):
        qi, ki, m, block_max = state
        n_blocks = (qi + 1) * per_tile
        last = ki == n_blocks - 1
        qi_next = jnp.where(last, qi + 1, qi)
        ki_next = jnp.where(last, 0, ki + 1)
        next_max = scores(qi_next, ki_next, s_nxt)
        m_old = jnp.where(ki == 0, MASK_VALUE, m)
        m_new = jnp.maximum(m_old, jnp.max(block_max, axis=0, keepdims=True))
        alpha = jnp.exp2(m_old - m_new)
        shifts = [jnp.broadcast_to(m_new, (SUBLANES, 2 * tq))] * SOFTMAX_CHAINS
        for g in range(tk // SOFTMAX_ROWS):
            rows = slice(g * SOFTMAX_ROWS, (g + 1) * SOFTMAX_ROWS)
            x = s_cur[rows, :].reshape(SOFTMAX_ROWS // SUBLANES, SUBLANES, 2 * tq)
            e = jnp.exp2(x - shifts[g % SOFTMAX_CHAINS]).reshape(SOFTMAX_ROWS, 2 * tq)
            p_cur[rows, :] = e.astype(BF16)
            shifts[g % SOFTMAX_CHAINS] = m_new + 0.0 * e[SOFTMAX_ROWS - SUBLANES:, :]
        start = pl.multiple_of(jnp.minimum(ki, seq // tk - 1) * tk, tk)
        vt_ext = jnp.concatenate([vt_ref[0, :, pl.ds(start, tk)], ones], axis=0)
        buf = lax.rem(qi, 2)
        for c in range(0, 2 * tq, MXU_COLS):
            cols = slice(c, c + MXU_COLS)
            acc_ref[buf, :, cols] = (alpha[:, cols] * acc_ref[buf, :, cols]
                                     + _dot(vt_ext, p_cur[:, cols]))
        done = jnp.where(last & (qi < n_q), qi, -1)
        return (qi_next, ki_next, m_new, next_max), done

    def body(_, state):
        completed = []
        for _ in range(ATTN_UNROLL // 2):
            state, done = sub_iteration(state, s0_ref, s1_ref, p0_ref)
            completed.append(done)
            state, done = sub_iteration(state, s1_ref, s0_ref, p1_ref)
            completed.append(done)
        for done in completed:
            pl.when(done >= 0)(functools.partial(finalize, done))
        return state

    acc_ref[...] = jnp.zeros_like(acc_ref)
    zero = jnp.int32(0)
    first_max = scores(zero, zero, s0_ref)
    state = (zero, zero, jnp.full((1, 2 * tq), MASK_VALUE, F32), first_max)
    lax.fori_loop(0, pl.cdiv(n_tasks, ATTN_UNROLL), body, state)


def _attention(k, qvt, lq1, lk1, lq2, lk2, subln_g, lambda_init):
    b, s, d = k.shape
    vec = lambda a: a.reshape(1, -1)
    head_rows = pl.BlockSpec((1, s, V_HEAD_DIM), lambda bi, h: (bi, 0, h))
    head_cols = lambda row: pl.BlockSpec((1, V_HEAD_DIM, s), lambda bi, h: (bi, row + h, 0))
    return pl.pallas_call(
        functools.partial(_attn_kernel, lambda_init=lambda_init),
        grid=(b, N_HEADS),
        in_specs=[head_cols(0), head_rows, head_cols(N_HEADS),
                  _const_spec((1, HEAD_DIM)), _const_spec((1, HEAD_DIM)),
                  _const_spec((1, HEAD_DIM)), _const_spec((1, HEAD_DIM)),
                  _const_spec((V_HEAD_DIM, 1))],
        out_specs=head_rows,
        out_shape=jax.ShapeDtypeStruct((b, s, d), BF16),
        scratch_shapes=[pltpu.VMEM((s // ATTN_Q, V_HEAD_DIM, 2 * ATTN_Q), BF16),
                        pltpu.VMEM((ATTN_K, V_HEAD_DIM), BF16),
                        pltpu.VMEM((ATTN_Q // ATTN_K + 1, V_HEAD_DIM, 2 * ATTN_Q), BF16),
                        pltpu.VMEM((ATTN_K, 2 * ATTN_Q), F32),
                        pltpu.VMEM((ATTN_K, 2 * ATTN_Q), F32),
                        pltpu.VMEM((ATTN_K, 2 * ATTN_Q), BF16),
                        pltpu.VMEM((ATTN_K, 2 * ATTN_Q), BF16),
                        pltpu.VMEM((2, V_HEAD_DIM + SUM_ROWS, 2 * ATTN_Q), F32)],
        compiler_params=_params(2),
        name="diff_attention",
    )(qvt, k, qvt, vec(lq1), vec(lk1), vec(lq2), vec(lk2), subln_g.reshape(-1, 1))


def _tail_kernel(*refs, has_proj, final_norm):
    refs = list(refs)
    h_ref = refs.pop(0)
    if has_proj:
        a_ref, wo_ref = refs.pop(0), refs.pop(0)
    fg_ref, wg_ref, wu_ref, wd_ref, p_ref, pg_ref, wpg_ref, wpp_ref, fin_ref, o_ref = refs
    x = h_ref[...]
    if has_proj:
        x = x + _dot(a_ref[...], wo_ref[...])
    xn = _rms(x, fg_ref[...]).astype(BF16)
    o_ref[...] = x
    for c in range(0, wg_ref.shape[1], FF_CHUNK):
        gate = _dot(xn, wg_ref[:, c:c + FF_CHUNK])
        up = _dot(xn, wu_ref[:, c:c + FF_CHUNK])
        act = (gate * jax.nn.sigmoid(gate) * up).astype(BF16)
        o_ref[...] += _dot(act, wd_ref[c:c + FF_CHUNK, :])
    y = o_ref[...]
    yn = _rms(y, pg_ref[...]).astype(BF16)
    y = y + _dot(p_ref[...].astype(BF16), wpp_ref[...]) * jax.nn.sigmoid(_dot(yn, wpg_ref[...]))
    if final_norm:
        y = _rms(y, fin_ref[...])
    o_ref[...] = y


def _layer_tail(h, proj, ffn_g, w_gate, w_up, w_down, p, ple_g, w_ple_gate, w_ple_proj,
                final_g, final_norm):
    t, d = h.shape
    f = w_gate.shape[1]
    e = p.shape[1]
    rows = lambda n: pl.BlockSpec((ROW_TILE, n), lambda i: (i, 0))
    resident = lambda shape: pl.BlockSpec(shape, lambda i: (0, 0), pipeline_mode=pl.Buffered(1))
    vec = lambda a: a.reshape(1, d)
    args, specs = [h], [rows(d)]
    if proj is not None:
        args += [proj[0], proj[1].astype(BF16)]
        specs += [rows(d), resident((d, d))]
    args += [vec(ffn_g), w_gate.astype(BF16), w_up.astype(BF16), w_down.astype(BF16), p,
             vec(ple_g), w_ple_gate.astype(BF16), w_ple_proj.astype(BF16), vec(final_g)]
    specs += [_const_spec((1, d)), resident((d, f)), resident((d, f)), resident((f, d)), rows(e),
              _const_spec((1, d)), resident((d, d)), resident((e, d)), _const_spec((1, d))]
    return pl.pallas_call(
        functools.partial(_tail_kernel, has_proj=proj is not None, final_norm=final_norm),
        grid=(t // ROW_TILE,),
        in_specs=specs,
        out_specs=rows(d),
        out_shape=jax.ShapeDtypeStruct((t, d), F32),
        compiler_params=_params(1),
        name="layer_tail",
    )(*args)


def _conv_kernel(u_ref, halo_ref, h_ref, dw_ref, dwb_ref, lng_ref, lnb_ref, w_ref, b_ref,
                 o_ref, ext_ref, shift_ref, y_ref):
    tile, d = u_ref.shape[1], u_ref.shape[2]
    first = pl.program_id(1) == 0
    ext_ref[0:CONV_HALO, :] = jnp.where(first, 0.0, halo_ref[0])
    ext_ref[CONV_HALO:, :] = u_ref[0]
    span = shift_ref.shape[1]
    for r in range(1, SUBLANES):
        shift_ref[r - 1] = ext_ref[r:r + span, :]
    lead = CONV_HALO - (CONV_WIDTH - 1)
    groups = CONV_ROWS // SUBLANES
    for c in range(0, d, CONV_COLS):
        cols = slice(c, c + CONV_COLS)
        for r0 in range(0, tile, CONV_ROWS):
            acc = jnp.zeros((groups, SUBLANES, CONV_COLS), F32) + dwb_ref[:, cols]
            for k in range(CONV_WIDTH):
                r = (lead + k) % SUBLANES
                a = r0 + lead + k - r
                x = ext_ref[a:a + CONV_ROWS, cols] if r == 0 else shift_ref[r - 1, a:a + CONV_ROWS, cols]
                acc = acc + x.reshape(groups, SUBLANES, CONV_COLS) * dw_ref[k, :, cols]
            y_ref[r0:r0 + CONV_ROWS, cols] = acc.reshape(CONV_ROWS, CONV_COLS)
    y = y_ref[...]
    mu = jnp.mean(y, axis=-1, keepdims=True)
    yc = y - mu
    y = yc * lax.rsqrt(jnp.mean(yc * yc, axis=-1, keepdims=True) + NORM_EPS) * lng_ref[...] + lnb_ref[...]
    y = (y * jax.nn.sigmoid(y)).astype(BF16)
    o_ref[0] = h_ref[0] + _dot(y, w_ref[...]) + b_ref[...]


def _conv_tail(u, h, dw_w, dw_b, ln_g, ln_b, pw2_w, pw2_b):
    b, s, d = u.shape
    per_tile = CONV_TILE // CONV_HALO
    vec = lambda a: a.reshape(1, d)
    return pl.pallas_call(
        _conv_kernel,
        grid=(b, s // CONV_TILE),
        in_specs=[pl.BlockSpec((1, CONV_TILE, d), lambda bi, i: (bi, i, 0)),
                  pl.BlockSpec((1, CONV_HALO, d),
                               lambda bi, i: (bi, jnp.maximum(i * per_tile - 1, 0), 0)),
                  pl.BlockSpec((1, CONV_TILE, d), lambda bi, i: (bi, i, 0)),
                  _const_spec((CONV_WIDTH, SUBLANES, d)), _const_spec((1, d)), _const_spec((1, d)),
                  _const_spec((1, d)), _const_spec((d, d)), _const_spec((1, d))],
        out_specs=pl.BlockSpec((1, CONV_TILE, d), lambda bi, i: (bi, i, 0)),
        out_shape=jax.ShapeDtypeStruct((b, s, d), F32),
        scratch_shapes=[pltpu.VMEM((CONV_TILE + CONV_HALO, d), F32),
                        pltpu.VMEM((SUBLANES - 1, CONV_TILE + CONV_HALO - SUBLANES, d), F32),
                        pltpu.VMEM((CONV_TILE, d), F32)],
        compiler_params=_params(2),
        name="conv_tail",
    )(u, u, h, jnp.broadcast_to(dw_w[:, None, :], (CONV_WIDTH, SUBLANES, d)), vec(dw_b),
      vec(ln_g), vec(ln_b), pw2_w.astype(BF16), vec(pw2_b))


def _lambda_init(layer_idx):
    return 0.8 - 0.6 * math.exp(-0.3 * layer_idx)


def kernel(x, p, attn_norm_g, w_qkv, lambda_q1, lambda_k1, lambda_q2, lambda_k2, subln_g, w_o, conv_norm_g, conv_pw1_w, conv_pw1_b, conv_dw_w, conv_dw_b, conv_ln_g, conv_ln_b, conv_pw2_w, conv_pw2_b, ffn_norm_g, w_gate, w_up, w_down, ple_norm_g, w_ple_gate, w_ple_proj, final_norm_g):
    b, s, d = x.shape
    depth = p.shape[0]
    t = b * s
    assert d == N_HEADS * V_HEAD_DIM and t % ROW_TILE == 0
    assert s % ATTN_Q == 0 and ATTN_Q % ATTN_K == 0 and ATTN_K % CHUNK == 0
    assert s % ROW_TILE == 0 and s % CONV_TILE == 0
    h = x.reshape(t, d)
    for i in range(depth):
        j = i // 2
        if i % 2 == 0:
            k, qvt = _norm_qkv(h, attn_norm_g[j], w_qkv[j], b, s)
            o = _attention(k.reshape(b, s, d), qvt, lambda_q1[j], lambda_k1[j], lambda_q2[j],
                           lambda_k2[j], subln_g[j], _lambda_init(i))
            proj = (o.reshape(t, d), w_o[j])
        else:
            u = _norm_glu(h, conv_norm_g[j], conv_pw1_w[j], conv_pw1_b[j])
            h = _conv_tail(u.reshape(b, s, d), h.reshape(b, s, d), conv_dw_w[j], conv_dw_b[j],
                           conv_ln_g[j], conv_ln_b[j], conv_pw2_w[j], conv_pw2_b[j]).reshape(t, d)
            proj = None
        h = _layer_tail(h, proj, ffn_norm_g[i], w_gate[i], w_up[i], w_down[i], p[i].reshape(t, -1),
                        ple_norm_g[i], w_ple_gate[i], w_ple_proj[i], final_norm_g,
                        final_norm=(i == depth - 1))
    return h.reshape(b, s, d)
```

```python
import functools
import math

import jax
import jax.numpy as jnp
from jax import lax
from jax.experimental import pallas as pl
from jax.experimental.pallas import tpu as pltpu

F32 = jnp.float32
BF16 = jnp.bfloat16

NORM_EPS = 1e-6
CHUNK = 64
N_HEADS = 8
HEAD_DIM = 64
V_HEAD_DIM = 2 * HEAD_DIM
CONV_WIDTH = 31
CONV_HALO = 32
MASK_VALUE = -1e30

SCORE_SCALE = HEAD_DIM ** -0.5 * math.log2(math.e)
SUM_ROWS = 16
SUBLANES = 8
MXU_COLS = 256
SOFTMAX_ROWS = 64
SOFTMAX_CHAINS = 2

ROW_TILE = 512
ATTN_Q = 512
ATTN_K = 512
ATTN_UNROLL = 2
CONV_TILE = 256
CONV_ROWS = 32
CONV_COLS = 512
COL_CHUNK = 512
FF_CHUNK = 256
VMEM_LIMIT = 56 * 1024 * 1024


def _params(n_axes):
    return pltpu.CompilerParams(dimension_semantics=("arbitrary",) * n_axes,
                                vmem_limit_bytes=VMEM_LIMIT)


def _rms(x, g):
    return x * lax.rsqrt(jnp.mean(x * x, axis=-1, keepdims=True) + NORM_EPS) * g


def _dot(a, b):
    return jnp.dot(a, b, preferred_element_type=F32)


def _const_spec(shape):
    return pl.BlockSpec(shape, lambda *_: (0,) * len(shape))


def _norm_qkv_kernel(x_ref, g_ref, wk_ref, wqvt_ref, k_ref, qvt_ref):
    xn = _rms(x_ref[...], g_ref[...]).astype(BF16)
    d = k_ref.shape[1]
    for c in range(0, d, COL_CHUNK):
        k_ref[:, c:c + COL_CHUNK] = _dot(xn, wk_ref[:, c:c + COL_CHUNK]).astype(k_ref.dtype)
    for c in range(0, 2 * d, COL_CHUNK):
        y = lax.dot_general(wqvt_ref[c:c + COL_CHUNK, :], xn, (((1,), (1,)), ((), ())),
                            preferred_element_type=F32)
        if c < d:
            y = y * SCORE_SCALE
        qvt_ref[0, c:c + COL_CHUNK, :] = y.astype(qvt_ref.dtype)


def _norm_qkv(h, g, w, b, s):
    t, d = h.shape
    per_seq = s // ROW_TILE
    w = w.astype(BF16)
    wqvt = jnp.concatenate([w[:, :d], w[:, 2 * d:]], axis=1).T
    return pl.pallas_call(
        _norm_qkv_kernel,
        grid=(t // ROW_TILE,),
        in_specs=[pl.BlockSpec((ROW_TILE, d), lambda i: (i, 0)),
                  _const_spec((1, d)), _const_spec((d, d)), _const_spec((2 * d, d))],
        out_specs=[pl.BlockSpec((ROW_TILE, d), lambda i: (i, 0)),
                   pl.BlockSpec((1, 2 * d, ROW_TILE), lambda i: (i // per_seq, 0, i % per_seq))],
        out_shape=[jax.ShapeDtypeStruct((t, d), BF16),
                   jax.ShapeDtypeStruct((b, 2 * d, s), BF16)],
        compiler_params=_params(1),
        name="norm_qkv",
    )(h, g.reshape(1, d), w[:, d:2 * d], wqvt)


def _norm_glu_kernel(x_ref, g_ref, w_ref, b_ref, o_ref):
    xn = _rms(x_ref[...], g_ref[...]).astype(BF16)
    n = o_ref.shape[1]
    for c in range(0, n, COL_CHUNK):
        a = _dot(xn, w_ref[:, c:c + COL_CHUNK]) + b_ref[:, c:c + COL_CHUNK]
        gt = _dot(xn, w_ref[:, n + c:n + c + COL_CHUNK]) + b_ref[:, n + c:n + c + COL_CHUNK]
        o_ref[:, c:c + COL_CHUNK] = a * jax.nn.sigmoid(gt)


def _norm_glu(h, g, w, b):
    t, d = h.shape
    n = w.shape[1] // 2
    return pl.pallas_call(
        _norm_glu_kernel,
        grid=(t // ROW_TILE,),
        in_specs=[pl.BlockSpec((ROW_TILE, d), lambda i: (i, 0)),
                  _const_spec((1, d)), _const_spec((d, 2 * n)), _const_spec((1, 2 * n))],
        out_specs=pl.BlockSpec((ROW_TILE, n), lambda i: (i, 0)),
        out_shape=jax.ShapeDtypeStruct((t, n), F32),
        compiler_params=_params(1),
        name="norm_glu",
    )(h, g.reshape(1, d), w.astype(BF16), b.reshape(1, 2 * n))


def _attn_kernel(qt_ref, k_ref, vt_ref, lq1_ref, lk1_ref, lq2_ref, lk2_ref, sg_ref, o_ref,
                 qq_ref, kx_ref, qx_ref, s0_ref, s1_ref, p0_ref, p1_ref, acc_ref, *, lambda_init):
    seq = k_ref.shape[1]
    tq, tk = ATTN_Q, ATTN_K
    n_q = seq // tq
    per_tile = tq // tk
    n_tasks = per_tile * n_q * (n_q + 1) // 2

    feature = lax.broadcasted_iota(jnp.int32, (V_HEAD_DIM, tq), 0)
    for i in range(n_q):
        q = qt_ref[0, :, i * tq:(i + 1) * tq]
        zero = jnp.zeros_like(q)
        qq_ref[i, :, :tq] = jnp.where(feature < HEAD_DIM, q, zero)
        qq_ref[i, :, tq:] = jnp.where(feature >= HEAD_DIM, q, zero)

    chunks = tk // CHUNK
    feat = lax.broadcasted_iota(jnp.int32, (tk, V_HEAD_DIM), 1)
    key_chunk = lax.broadcasted_iota(jnp.int32, (tk, V_HEAD_DIM), 0) // CHUNK
    kx_ref[...] = jnp.where(feat == key_chunk, 1.0, 0.0).astype(BF16)
    feat = lax.broadcasted_iota(jnp.int32, (V_HEAD_DIM, 2 * tq), 0)
    query_chunk = (lax.broadcasted_iota(jnp.int32, (V_HEAD_DIM, 2 * tq), 1) % tq) // CHUNK
    qx_ref[0] = jnp.zeros((V_HEAD_DIM, 2 * tq), BF16)
    for r in range(per_tile):
        hidden = (feat < chunks) & (feat + r * chunks > query_chunk)
        qx_ref[r + 1] = jnp.where(hidden, MASK_VALUE, 0.0).astype(BF16)

    ones = jnp.ones((SUM_ROWS, tk), BF16)
    lam = (jnp.exp(jnp.sum(lq1_ref[...] * lk1_ref[...], keepdims=True))
           - jnp.exp(jnp.sum(lq2_ref[...] * lk2_ref[...], keepdims=True)) + lambda_init)
    gain = sg_ref[...] * (1.0 - lambda_init)

    def scores(qi, ki, s_ref):
        qi = jnp.minimum(qi, n_q - 1)
        start = pl.multiple_of(jnp.minimum(ki, seq // tk - 1) * tk, tk)
        slot = jnp.clip(ki - qi * per_tile + 1, 0, per_tile)
        k_ext = jnp.concatenate([k_ref[0, pl.ds(start, tk), :], kx_ref[...]], axis=1)
        maxes = []
        for c in range(0, 2 * tq, MXU_COLS):
            q_ext = jnp.concatenate([qq_ref[qi, :, c:c + MXU_COLS],
                                     qx_ref[slot, :, c:c + MXU_COLS]], axis=0)
            s = _dot(k_ext, q_ext)
            s_ref[:, c:c + MXU_COLS] = s
            maxes.append(jnp.max(s.reshape(tk // SUBLANES, SUBLANES, MXU_COLS), axis=0))
        return jnp.concatenate(maxes, axis=1)

    def finalize(qi):
        buf = lax.rem(qi, 2)
        r = 1.0 / acc_ref[buf, V_HEAD_DIM:V_HEAD_DIM + 1, :]
        o = (acc_ref[buf, :V_HEAD_DIM, :tq] * r[:, :tq]
             - acc_ref[buf, :V_HEAD_DIM, tq:] * (lam * r[:, tq:]))
        o = o * lax.rsqrt(jnp.mean(o * o, axis=0, keepdims=True) + NORM_EPS) * gain
        o_ref[0, pl.ds(pl.multiple_of(qi * tq, tq), tq), :] = o.T.astype(o_ref.dtype)

    def sub_iteration(state, s_cur, s_nxt, p_cur):
        qi, ki, m, block_max = state
        n_blocks = (qi + 1) * per_tile
        last = ki == n_blocks - 1
        qi_next = jnp.where(last, qi + 1, qi)
        ki_next = jnp.where(last, 0, ki + 1)
        next_max = scores(qi_next, ki_next, s_nxt)
        m_old = jnp.where(ki == 0, MASK_VALUE, m)
        m_new = jnp.maximum(m_old, jnp.max(block_max, axis=0, keepdims=True))
        alpha = jnp.exp2(m_old - m_new)
        shifts = [jnp.broadcast_to(m_new, (SUBLANES, 2 * tq))] * SOFTMAX_CHAINS
        for g in range(tk // SOFTMAX_ROWS):
            rows = slice(g * SOFTMAX_ROWS, (g + 1) * SOFTMAX_ROWS)
            x = s_cur[rows, :].reshape(SOFTMAX_ROWS // SUBLANES, SUBLANES, 2 * tq)
            e = jnp.exp2(x - shifts[g % SOFTMAX_CHAINS]).reshape(SOFTMAX_ROWS, 2 * tq)
            p_cur[rows, :] = e.astype(BF16)
            shifts[g % SOFTMAX_CHAINS] = m_new + 0.0 * e[SOFTMAX_ROWS - SUBLANES:, :]
        start = pl.multiple_of(jnp.minimum(ki, seq // tk - 1) * tk, tk)
        vt_ext = jnp.concatenate([vt_ref[0, :, pl.ds(start, tk)], ones], axis=0)
        buf = lax.rem(qi, 2)
        for c in range(0, 2 * tq, MXU_COLS):
            cols = slice(c, c + MXU_COLS)
            acc_ref[buf, :, cols] = (alpha[:, cols] * acc_ref[buf, :, cols]
                                     + _dot(vt_ext, p_cur[:, cols]))
        done = jnp.where(last & (qi < n_q), qi, -1)
        return (qi_next, ki_next, m_new, next_max), done

    def body(_, state):
        assert ATTN_UNROLL == 2
        completed = []
        for _ in range(ATTN_UNROLL // 2):
            state, done = sub_iteration(state, s0_ref, s1_ref, p0_ref)
            completed.append(done)
            state, done = sub_iteration(state, s1_ref, s0_ref, p1_ref)
            completed.append(done)
        for done in completed:
            pl.when(done >= 0)(functools.partial(finalize, done))
        return state

    acc_ref[...] = jnp.zeros_like(acc_ref)
    zero = jnp.int32(0)
    first_max = scores(zero, zero, s0_ref)
    state = (zero, zero, jnp.full((1, 2 * tq), MASK_VALUE, F32), first_max)
    lax.fori_loop(0, pl.cdiv(n_tasks, ATTN_UNROLL), body, state)


def _attention(k, qvt, lq1, lk1, lq2, lk2, subln_g, lambda_init):
    b, s, d = k.shape
    vec = lambda a: a.reshape(1, -1)
    head_rows = pl.BlockSpec((1, s, V_HEAD_DIM), lambda bi, h: (bi, 0, h))
    head_cols = lambda row: pl.BlockSpec((1, V_HEAD_DIM, s), lambda bi, h: (bi, row + h, 0))
    return pl.pallas_call(
        functools.partial(_attn_kernel, lambda_init=lambda_init),
        grid=(b, N_HEADS),
        in_specs=[head_cols(0), head_rows, head_cols(N_HEADS),
                  _const_spec((1, HEAD_DIM)), _const_spec((1, HEAD_DIM)),
                  _const_spec((1, HEAD_DIM)), _const_spec((1, HEAD_DIM)),
                  _const_spec((V_HEAD_DIM, 1))],
        out_specs=head_rows,
        out_shape=jax.ShapeDtypeStruct((b, s, d), BF16),
        scratch_shapes=[pltpu.VMEM((s // ATTN_Q, V_HEAD_DIM, 2 * ATTN_Q), BF16),
                        pltpu.VMEM((ATTN_K, V_HEAD_DIM), BF16),
                        pltpu.VMEM((ATTN_Q // ATTN_K + 1, V_HEAD_DIM, 2 * ATTN_Q), BF16),
                        pltpu.VMEM((ATTN_K, 2 * ATTN_Q), F32),
                        pltpu.VMEM((ATTN_K, 2 * ATTN_Q), F32),
                        pltpu.VMEM((ATTN_K, 2 * ATTN_Q), BF16),
                        pltpu.VMEM((ATTN_K, 2 * ATTN_Q), BF16),
                        pltpu.VMEM((2, V_HEAD_DIM + SUM_ROWS, 2 * ATTN_Q), F32)],
        compiler_params=_params(2),
        name="diff_attention",
    )(qvt, k, qvt, vec(lq1), vec(lk1), vec(lq2), vec(lk2), subln_g.reshape(-1, 1))


def _tail_kernel(*refs, has_proj, final_norm):
    refs = list(refs)
    h_ref = refs.pop(0)
    if has_proj:
        a_ref, wo_ref = refs.pop(0), refs.pop(0)
    fg_ref, wg_ref, wu_ref, wd_ref, p_ref, pg_ref, wpg_ref, wpp_ref, fin_ref, o_ref = refs
    x = h_ref[...]
    if has_proj:
        x = x + _dot(a_ref[...], wo_ref[...])
    xn = _rms(x, fg_ref[...]).astype(BF16)
    o_ref[...] = x
    for c in range(0, wg_ref.shape[1], FF_CHUNK):
        gate = _dot(xn, wg_ref[:, c:c + FF_CHUNK])
        up = _dot(xn, wu_ref[:, c:c + FF_CHUNK])
        act = (gate * jax.nn.sigmoid(gate) * up).astype(BF16)
        o_ref[...] += _dot(act, wd_ref[c:c + FF_CHUNK, :])
    y = o_ref[...]
    yn = _rms(y, pg_ref[...]).astype(BF16)
    y = y + _dot(p_ref[...].astype(BF16), wpp_ref[...]) * jax.nn.sigmoid(_dot(yn, wpg_ref[...]))
    if final_norm:
        y = _rms(y, fin_ref[...])
    o_ref[...] = y


def _layer_tail(h, proj, ffn_g, w_gate, w_up, w_down, p, ple_g, w_ple_gate, w_ple_proj,
                final_g, final_norm):
    t, d = h.shape
    f = w_gate.shape[1]
    e = p.shape[1]
    rows = lambda n: pl.BlockSpec((ROW_TILE, n), lambda i: (i, 0))
    resident = lambda shape: pl.BlockSpec(shape, lambda i: (0, 0), pipeline_mode=pl.Buffered(1))
    vec = lambda a: a.reshape(1, d)
    args, specs = [h], [rows(d)]
    if proj is not None:
        args += [proj[0], proj[1].astype(BF16)]
        specs += [rows(d), resident((d, d))]
    args += [vec(ffn_g), w_gate.astype(BF16), w_up.astype(BF16), w_down.astype(BF16), p,
             vec(ple_g), w_ple_gate.astype(BF16), w_ple_proj.astype(BF16), vec(final_g)]
    specs += [_const_spec((1, d)), resident((d, f)), resident((d, f)), resident((f, d)), rows(e),
              _const_spec((1, d)), resident((d, d)), resident((e, d)), _const_spec((1, d))]
    return pl.pallas_call(
        functools.partial(_tail_kernel, has_proj=proj is not None, final_norm=final_norm),
        grid=(t // ROW_TILE,),
        in_specs=specs,
        out_specs=rows(d),
        out_shape=jax.ShapeDtypeStruct((t, d), F32),
        compiler_params=_params(1),
        name="layer_tail",
    )(*args)


def _conv_kernel(u_ref, halo_ref, h_ref, dw_ref, dwb_ref, lng_ref, lnb_ref, w_ref, b_ref,
                 o_ref, ext_ref, shift_ref, y_ref):
    tile, d = u_ref.shape[1], u_ref.shape[2]
    first = pl.program_id(1) == 0
    ext_ref[0:CONV_HALO, :] = jnp.where(first, 0.0, halo_ref[0])
    ext_ref[CONV_HALO:, :] = u_ref[0]
    span = shift_ref.shape[1]
    for r in range(1, SUBLANES):
        shift_ref[r - 1] = ext_ref[r:r + span, :]
    lead = CONV_HALO - (CONV_WIDTH - 1)
    groups = CONV_ROWS // SUBLANES
    for c in range(0, d, CONV_COLS):
        cols = slice(c, c + CONV_COLS)
        for r0 in range(0, tile, CONV_ROWS):
            acc = jnp.zeros((groups, SUBLANES, CONV_COLS), F32) + dwb_ref[:, cols]
            for k in range(CONV_WIDTH):
                r = (lead + k) % SUBLANES
                a = r0 + lead + k - r
                x = ext_ref[a:a + CONV_ROWS, cols] if r == 0 else shift_ref[r - 1, a:a + CONV_ROWS, cols]
                acc = acc + x.reshape(groups, SUBLANES, CONV_COLS) * dw_ref[k, :, cols]
            y_ref[r0:r0 + CONV_ROWS, cols] = acc.reshape(CONV_ROWS, CONV_COLS)
    y = y_ref[...]
    mu = jnp.mean(y, axis=-1, keepdims=True)
    yc = y - mu
    y = yc * lax.rsqrt(jnp.mean(yc * yc, axis=-1, keepdims=True) + NORM_EPS) * lng_ref[...] + lnb_ref[...]
    y = (y * jax.nn.sigmoid(y)).astype(BF16)
    o_ref[0] = h_ref[0] + _dot(y, w_ref[...]) + b_ref[...]


def _conv_tail(u, h, dw_w, dw_b, ln_g, ln_b, pw2_w, pw2_b):
    b, s, d = u.shape
    per_tile = CONV_TILE // CONV_HALO
    vec = lambda a: a.reshape(1, d)
    return pl.pallas_call(
        _conv_kernel,
        grid=(b, s // CONV_TILE),
        in_specs=[pl.BlockSpec((1, CONV_TILE, d), lambda bi, i: (bi, i, 0)),
                  pl.BlockSpec((1, CONV_HALO, d),
                               lambda bi, i: (bi, jnp.maximum(i * per_tile - 1, 0), 0)),
                  pl.BlockSpec((1, CONV_TILE, d), lambda bi, i: (bi, i, 0)),
                  _const_spec((CONV_WIDTH, SUBLANES, d)), _const_spec((1, d)), _const_spec((1, d)),
                  _const_spec((1, d)), _const_spec((d, d)), _const_spec((1, d))],
        out_specs=pl.BlockSpec((1, CONV_TILE, d), lambda bi, i: (bi, i, 0)),
        out_shape=jax.ShapeDtypeStruct((b, s, d), F32),
        scratch_shapes=[pltpu.VMEM((CONV_TILE + CONV_HALO, d), F32),
                        pltpu.VMEM((SUBLANES - 1, CONV_TILE + CONV_HALO - SUBLANES, d), F32),
                        pltpu.VMEM((CONV_TILE, d), F32)],
        compiler_params=_params(2),
        name="conv_tail",
    )(u, u, h, jnp.broadcast_to(dw_w[:, None, :], (CONV_WIDTH, SUBLANES, d)), vec(dw_b),
      vec(ln_g), vec(ln_b), pw2_w.astype(BF16), vec(pw2_b))


def _lambda_init(layer_idx):
    return 0.8 - 0.6 * math.exp(-0.3 * layer_idx)


def kernel(x, p, attn_norm_g, w_qkv, lambda_q1, lambda_k1, lambda_q2, lambda_k2, subln_g, w_o, conv_norm_g, conv_pw1_w, conv_pw1_b, conv_dw_w, conv_dw_b, conv_ln_g, conv_ln_b, conv_pw2_w, conv_pw2_b, ffn_norm_g, w_gate, w_up, w_down, ple_norm_g, w_ple_gate, w_ple_proj, final_norm_g):
    b, s, d = x.shape
    depth = p.shape[0]
    t = b * s
    assert d == N_HEADS * V_HEAD_DIM and t % ROW_TILE == 0
    assert s % ATTN_Q == 0 and ATTN_Q % ATTN_K == 0 and ATTN_K % CHUNK == 0
    assert s % ROW_TILE == 0 and s % CONV_TILE == 0
    h = x.reshape(t, d)
    for i in range(depth):
        j = i // 2
        if i % 2 == 0:
            k, qvt = _norm_qkv(h, attn_norm_g[j], w_qkv[j], b, s)
            o = _attention(k.reshape(b, s, d), qvt, lambda_q1[j], lambda_k1[j], lambda_q2[j],
                           lambda_k2[j], subln_g[j], _lambda_init(i))
            proj = (o.reshape(t, d), w_o[j])
        else:
            u = _norm_glu(h, conv_norm_g[j], conv_pw1_w[j], conv_pw1_b[j])
            h = _conv_tail(u.reshape(b, s, d), h.reshape(b, s, d), conv_dw_w[j], conv_dw_b[j],
                           conv_ln_g[j], conv_ln_b[j], conv_pw2_w[j], conv_pw2_b[j]).reshape(t, d)
            proj = None
        h = _layer_tail(h, proj, ffn_norm_g[i], w_gate[i], w_up[i], w_down[i], p[i].reshape(t, -1),
                        ple_norm_g[i], w_ple_gate[i], w_ple_proj[i], final_norm_g,
                        final_norm=(i == depth - 1))
    return h.reshape(b, s, d)
```

```python
import functools
import math

import jax
import jax.numpy as jnp
from jax import lax
from jax.experimental import pallas as pl
from jax.experimental.pallas import tpu as pltpu

F32 = jnp.float32
BF16 = jnp.bfloat16

NORM_EPS = 1e-6
CHUNK = 64
N_HEADS = 8
HEAD_DIM = 64
V_HEAD_DIM = 2 * HEAD_DIM
CONV_WIDTH = 31
CONV_HALO = 32
MASK_VALUE = -1e30

SCORE_SCALE = HEAD_DIM ** -0.5 * math.log2(math.e)
SUM_ROWS = 16
SUBLANES = 8
MXU_COLS = 256
SOFTMAX_ROWS = 64
SOFTMAX_CHAINS = 2

ROW_TILE = 1024
ATTN_Q = 512
ATTN_K = 512
ATTN_UNROLL = 2
CONV_TILE = 512
CONV_ROWS = 32
CONV_COLS = 512
COL_CHUNK = 512
FF_CHUNK = 256
VMEM_LIMIT = 56 * 1024 * 1024


def _params(n_axes):
    return pltpu.CompilerParams(dimension_semantics=("arbitrary",) * n_axes,
                                vmem_limit_bytes=VMEM_LIMIT)


def _rms(x, g):
    return x * lax.rsqrt(jnp.mean(x * x, axis=-1, keepdims=True) + NORM_EPS) * g


def _dot(a, b):
    return jnp.dot(a, b, preferred_element_type=F32)


def _const_spec(shape):
    return pl.BlockSpec(shape, lambda *_: (0,) * len(shape))


def _norm_qkv_kernel(x_ref, g_ref, wk_ref, wqvt_ref, k_ref, qvt_ref):
    xn = _rms(x_ref[...], g_ref[...]).astype(BF16)
    d = k_ref.shape[1]
    for c in range(0, d, COL_CHUNK):
        k_ref[:, c:c + COL_CHUNK] = _dot(xn, wk_ref[:, c:c + COL_CHUNK]).astype(k_ref.dtype)
    for c in range(0, 2 * d, COL_CHUNK):
        y = lax.dot_general(wqvt_ref[c:c + COL_CHUNK, :], xn, (((1,), (1,)), ((), ())),
                            preferred_element_type=F32)
        if c < d:
            y = y * SCORE_SCALE
        qvt_ref[0, c:c + COL_CHUNK, :] = y.astype(qvt_ref.dtype)


def _norm_qkv(h, g, w, b, s):
    t, d = h.shape
    per_seq = s // ROW_TILE
    w = w.astype(BF16)
    wqvt = jnp.concatenate([w[:, :d], w[:, 2 * d:]], axis=1).T
    return pl.pallas_call(
        _norm_qkv_kernel,
        grid=(t // ROW_TILE,),
        in_specs=[pl.BlockSpec((ROW_TILE, d), lambda i: (i, 0)),
                  _const_spec((1, d)), _const_spec((d, d)), _const_spec((2 * d, d))],
        out_specs=[pl.BlockSpec((ROW_TILE, d), lambda i: (i, 0)),
                   pl.BlockSpec((1, 2 * d, ROW_TILE), lambda i: (i // per_seq, 0, i % per_seq))],
        out_shape=[jax.ShapeDtypeStruct((t, d), BF16),
                   jax.ShapeDtypeStruct((b, 2 * d, s), BF16)],
        compiler_params=_params(1),
        name="norm_qkv",
    )(h, g.reshape(1, d), w[:, d:2 * d], wqvt)


def _norm_glu_kernel(x_ref, g_ref, w_ref, b_ref, o_ref):
    xn = _rms(x_ref[...], g_ref[...]).astype(BF16)
    n = o_ref.shape[1]
    for c in range(0, n, COL_CHUNK):
        a = _dot(xn, w_ref[:, c:c + COL_CHUNK]) + b_ref[:, c:c + COL_CHUNK]
        gt = _dot(xn, w_ref[:, n + c:n + c + COL_CHUNK]) + b_ref[:, n + c:n + c + COL_CHUNK]
        o_ref[:, c:c + COL_CHUNK] = a * jax.nn.sigmoid(gt)


def _norm_glu(h, g, w, b):
    t, d = h.shape
    n = w.shape[1] // 2
    return pl.pallas_call(
        _norm_glu_kernel,
        grid=(t // ROW_TILE,),
        in_specs=[pl.BlockSpec((ROW_TILE, d), lambda i: (i, 0)),
                  _const_spec((1, d)), _const_spec((d, 2 * n)), _const_spec((1, 2 * n))],
        out_specs=pl.BlockSpec((ROW_TILE, n), lambda i: (i, 0)),
        out_shape=jax.ShapeDtypeStruct((t, n), F32),
        compiler_params=_params(1),
        name="norm_glu",
    )(h, g.reshape(1, d), w.astype(BF16), b.reshape(1, 2 * n))


def _attn_kernel(qt_ref, k_ref, vt_ref, lq1_ref, lk1_ref, lq2_ref, lk2_ref, sg_ref, o_ref,
                 qq_ref, kx_ref, qx_ref, s0_ref, s1_ref, p0_ref, p1_ref, acc_ref, *, lambda_init):
    seq = k_ref.shape[1]
    tq, tk = ATTN_Q, ATTN_K
    n_q = seq // tq
    per_tile = tq // tk
    n_tasks = per_tile * n_q * (n_q + 1) // 2

    feature = lax.broadcasted_iota(jnp.int32, (V_HEAD_DIM, tq), 0)
    for i in range(n_q):
        q = qt_ref[0, :, i * tq:(i + 1) * tq]
        zero = jnp.zeros_like(q)
        qq_ref[i, :, :tq] = jnp.where(feature < HEAD_DIM, q, zero)
        qq_ref[i, :, tq:] = jnp.where(feature >= HEAD_DIM, q, zero)

    chunks = tk // CHUNK
    feat = lax.broadcasted_iota(jnp.int32, (tk, V_HEAD_DIM), 1)
    key_chunk = lax.broadcasted_iota(jnp.int32, (tk, V_HEAD_DIM), 0) // CHUNK
    kx_ref[...] = jnp.where(feat == key_chunk, 1.0, 0.0).astype(BF16)
    feat = lax.broadcasted_iota(jnp.int32, (V_HEAD_DIM, 2 * tq), 0)
    query_chunk = (lax.broadcasted_iota(jnp.int32, (V_HEAD_DIM, 2 * tq), 1) % tq) // CHUNK
    qx_ref[0] = jnp.zeros((V_HEAD_DIM, 2 * tq), BF16)
    for r in range(per_tile):
        hidden = (feat < chunks) & (feat + r * chunks > query_chunk)
        qx_ref[r + 1] = jnp.where(hidden, MASK_VALUE, 0.0).astype(BF16)

    ones = jnp.ones((SUM_ROWS, tk), BF16)
    lam = (jnp.exp(jnp.sum(lq1_ref[...] * lk1_ref[...], keepdims=True))
           - jnp.exp(jnp.sum(lq2_ref[...] * lk2_ref[...], keepdims=True)) + lambda_init)
    gain = sg_ref[...] * (1.0 - lambda_init)

    def scores(qi, ki, s_ref):
        qi = jnp.minimum(qi, n_q - 1)
        start = pl.multiple_of(jnp.minimum(ki, seq // tk - 1) * tk, tk)
        slot = jnp.clip(ki - qi * per_tile + 1, 0, per_tile)
        k_ext = jnp.concatenate([k_ref[0, pl.ds(start, tk), :], kx_ref[...]], axis=1)
        maxes = []
        for c in range(0, 2 * tq, MXU_COLS):
            q_ext = jnp.concatenate([qq_ref[qi, :, c:c + MXU_COLS],
                                     qx_ref[slot, :, c:c + MXU_COLS]], axis=0)
            s = _dot(k_ext, q_ext)
            s_ref[:, c:c + MXU_COLS] = s
            maxes.append(jnp.max(s.reshape(tk // SUBLANES, SUBLANES, MXU_COLS), axis=0))
        return jnp.concatenate(maxes, axis=1)

    def finalize(qi):
        buf = lax.rem(qi, 2)
        r = 1.0 / acc_ref[buf, V_HEAD_DIM:V_HEAD_DIM + 1, :]
        o = (acc_ref[buf, :V_HEAD_DIM, :tq] * r[:, :tq]
             - acc_ref[buf, :V_HEAD_DIM, tq:] * (lam * r[:, tq:]))
        o = o * lax.rsqrt(jnp.mean(o * o, axis=0, keepdims=True) + NORM_EPS) * gain
        o_ref[0, pl.ds(pl.multiple_of(qi * tq, tq), tq), :] = o.T.astype(o_ref.dtype)

    def sub_iteration(state, s_cur, s_nxt, p_cur):
        qi, ki, m, block_max = state
        n_blocks = (qi + 1) * per_tile
        last = ki == n_blocks - 1
        qi_next = jnp.where(last, qi + 1, qi)
        ki_next = jnp.where(last, 0, ki + 1)
        next_max = scores(qi_next, ki_next, s_nxt)
        m_old = jnp.where(ki == 0, MASK_VALUE, m)
        m_new = jnp.maximum(m_old, jnp.max(block_max, axis=0, keepdims=True))
        alpha = jnp.exp2(m_old - m_new)
        shifts = [jnp.broadcast_to(m_new, (SUBLANES, 2 * tq))] * SOFTMAX_CHAINS
        for g in range(tk // SOFTMAX_ROWS):
            rows = slice(g * SOFTMAX_ROWS, (g + 1) * SOFTMAX_ROWS)
            x = s_cur[rows, :].reshape(SOFTMAX_ROWS // SUBLANES, SUBLANES, 2 * tq)
            e = jnp.exp2(x - shifts[g % SOFTMAX_CHAINS]).reshape(SOFTMAX_ROWS, 2 * tq)
            p_cur[rows, :] = e.astype(BF16)
            shifts[g % SOFTMAX_CHAINS] = m_new + 0.0 * e[SOFTMAX_ROWS - SUBLANES:, :]
        start = pl.multiple_of(jnp.minimum(ki, seq // tk - 1) * tk, tk)
        vt_ext = jnp.concatenate([vt_ref[0, :, pl.ds(start, tk)], ones], axis=0)
        buf = lax.rem(qi, 2)
        for c in range(0, 2 * tq, MXU_COLS):
            cols = slice(c, c + MXU_COLS)
            acc_ref[buf, :, cols] = (alpha[:, cols] * acc_ref[buf, :, cols]
                                     + _dot(vt_ext, p_cur[:, cols]))
        done = jnp.where(last & (qi < n_q), qi, -1)
        return (qi_next, ki_next, m_new, next_max), done

    def body(_, state):
        assert ATTN_UNROLL == 2
        completed = []
        for _ in range(ATTN_UNROLL // 2):
            state, done = sub_iteration(state, s0_ref, s1_ref, p0_ref)
            completed.append(done)
            state, done = sub_iteration(state, s1_ref, s0_ref, p1_ref)
            completed.append(done)
        for done in completed:
            pl.when(done >= 0)(functools.partial(finalize, done))
        return state

    acc_ref[...] = jnp.zeros_like(acc_ref)
    zero = jnp.int32(0)
    first_max = scores(zero, zero, s0_ref)
    state = (zero, zero, jnp.full((1, 2 * tq), MASK_VALUE, F32), first_max)
    lax.fori_loop(0, pl.cdiv(n_tasks, ATTN_UNROLL), body, state)


def _attention(k, qvt, lq1, lk1, lq2, lk2, subln_g, lambda_init):
    b, s, d = k.shape
    vec = lambda a: a.reshape(1, -1)
    head_rows = pl.BlockSpec((1, s, V_HEAD_DIM), lambda bi, h: (bi, 0, h))
    head_cols = lambda row: pl.BlockSpec((1, V_HEAD_DIM, s), lambda bi, h: (bi, row + h, 0))
    return pl.pallas_call(
        functools.partial(_attn_kernel, lambda_init=lambda_init),
        grid=(b, N_HEADS),
        in_specs=[head_cols(0), head_rows, head_cols(N_HEADS),
                  _const_spec((1, HEAD_DIM)), _const_spec((1, HEAD_DIM)),
                  _const_spec((1, HEAD_DIM)), _const_spec((1, HEAD_DIM)),
                  _const_spec((V_HEAD_DIM, 1))],
        out_specs=head_rows,
        out_shape=jax.ShapeDtypeStruct((b, s, d), BF16),
        scratch_shapes=[pltpu.VMEM((s // ATTN_Q, V_HEAD_DIM, 2 * ATTN_Q), BF16),
                        pltpu.VMEM((ATTN_K, V_HEAD_DIM), BF16),
                        pltpu.VMEM((ATTN_Q // ATTN_K + 1, V_HEAD_DIM, 2 * ATTN_Q), BF16),
                        pltpu.VMEM((ATTN_K, 2 * ATTN_Q), F32),
                        pltpu.VMEM((ATTN_K, 2 * ATTN_Q), F32),
                        pltpu.VMEM((ATTN_K, 2 * ATTN_Q), BF16),
                        pltpu.VMEM((ATTN_K, 2 * ATTN_Q), BF16),
                        pltpu.VMEM((2, V_HEAD_DIM + SUM_ROWS, 2 * ATTN_Q), F32)],
        compiler_params=_params(2),
        name="diff_attention",
    )(qvt, k, qvt, vec(lq1), vec(lk1), vec(lq2), vec(lk2), subln_g.reshape(-1, 1))


def _tail_kernel(*refs, has_proj, final_norm):
    refs = list(refs)
    h_ref = refs.pop(0)
    if has_proj:
        a_ref, wo_ref = refs.pop(0), refs.pop(0)
    fg_ref, wg_ref, wu_ref, wd_ref, p_ref, pg_ref, wpg_ref, wpp_ref, fin_ref, o_ref = refs
    x = h_ref[...]
    if has_proj:
        x = x + _dot(a_ref[...], wo_ref[...])
    xn = _rms(x, fg_ref[...]).astype(BF16)
    o_ref[...] = x
    for c in range(0, wg_ref.shape[1], FF_CHUNK):
        gate = _dot(xn, wg_ref[:, c:c + FF_CHUNK])
        up = _dot(xn, wu_ref[:, c:c + FF_CHUNK])
        act = (gate * jax.nn.sigmoid(gate) * up).astype(BF16)
        o_ref[...] += _dot(act, wd_ref[c:c + FF_CHUNK, :])
    y = o_ref[...]
    yn = _rms(y, pg_ref[...]).astype(BF16)
    y = y + _dot(p_ref[...].astype(BF16), wpp_ref[...]) * jax.nn.sigmoid(_dot(yn, wpg_ref[...]))
    if final_norm:
        y = _rms(y, fin_ref[...])
    o_ref[...] = y


def _layer_tail(h, proj, ffn_g, w_gate, w_up, w_down, p, ple_g, w_ple_gate, w_ple_proj,
                final_g, final_norm):
    t, d = h.shape
    f = w_gate.shape[1]
    e = p.shape[1]
    rows = lambda n: pl.BlockSpec((ROW_TILE, n), lambda i: (i, 0))
    resident = lambda shape: pl.BlockSpec(shape, lambda i: (0, 0), pipeline_mode=pl.Buffered(1))
    vec = lambda a: a.reshape(1, d)
    args, specs = [h], [rows(d)]
    if proj is not None:
        args += [proj[0], proj[1].astype(BF16)]
        specs += [rows(d), resident((d, d))]
    args += [vec(ffn_g), w_gate.astype(BF16), w_up.astype(BF16), w_down.astype(BF16), p,
             vec(ple_g), w_ple_gate.astype(BF16), w_ple_proj.astype(BF16), vec(final_g)]
    specs += [_const_spec((1, d)), resident((d, f)), resident((d, f)), resident((f, d)), rows(e),
              _const_spec((1, d)), resident((d, d)), resident((e, d)), _const_spec((1, d))]
    return pl.pallas_call(
        functools.partial(_tail_kernel, has_proj=proj is not None, final_norm=final_norm),
        grid=(t // ROW_TILE,),
        in_specs=specs,
        out_specs=rows(d),
        out_shape=jax.ShapeDtypeStruct((t, d), F32),
        compiler_params=_params(1),
        name="layer_tail",
    )(*args)


def _conv_kernel(u_ref, halo_ref, h_ref, dw_ref, dwb_ref, lng_ref, lnb_ref, w_ref, b_ref,
                 o_ref, ext_ref, shift_ref, y_ref):
    tile, d = u_ref.shape[1], u_ref.shape[2]
    first = pl.program_id(1) == 0
    ext_ref[0:CONV_HALO, :] = jnp.where(first, 0.0, halo_ref[0])
    ext_ref[CONV_HALO:, :] = u_ref[0]
    span = shift_ref.shape[1]
    for r in range(1, SUBLANES):
        shift_ref[r - 1] = ext_ref[r:r + span, :]
    lead = CONV_HALO - (CONV_WIDTH - 1)
    groups = CONV_ROWS // SUBLANES
    for c in range(0, d, CONV_COLS):
        cols = slice(c, c + CONV_COLS)
        for r0 in range(0, tile, CONV_ROWS):
            acc = jnp.zeros((groups, SUBLANES, CONV_COLS), F32) + dwb_ref[:, cols]
            for k in range(CONV_WIDTH):
                r = (lead + k) % SUBLANES
                a = r0 + lead + k - r
                x = ext_ref[a:a + CONV_ROWS, cols] if r == 0 else shift_ref[r - 1, a:a + CONV_ROWS, cols]
                acc = acc + x.reshape(groups, SUBLANES, CONV_COLS) * dw_ref[k, :, cols]
            y_ref[r0:r0 + CONV_ROWS, cols] = acc.reshape(CONV_ROWS, CONV_COLS)
    y = y_ref[...]
    mu = jnp.mean(y, axis=-1, keepdims=True)
    yc = y - mu
    y = yc * lax.rsqrt(jnp.mean(yc * yc, axis=-1, keepdims=True) + NORM_EPS) * lng_ref[...] + lnb_ref[...]
    y = (y * jax.nn.sigmoid(y)).astype(BF16)
    o_ref[0] = h_ref[0] + _dot(y, w_ref[...]) + b_ref[...]


def _conv_tail(u, h, dw_w, dw_b, ln_g, ln_b, pw2_w, pw2_b):
    b, s, d = u.shape
    per_tile = CONV_TILE // CONV_HALO
    vec = lambda a: a.reshape(1, d)
    return pl.pallas_call(
        _conv_kernel,
        grid=(b, s // CONV_TILE),
        in_specs=[pl.BlockSpec((1, CONV_TILE, d), lambda bi, i: (bi, i, 0)),
                  pl.BlockSpec((1, CONV_HALO, d),
                               lambda bi, i: (bi, jnp.maximum(i * per_tile - 1, 0), 0)),
                  pl.BlockSpec((1, CONV_TILE, d), lambda bi, i: (bi, i, 0)),
                  _const_spec((CONV_WIDTH, SUBLANES, d)), _const_spec((1, d)), _const_spec((1, d)),
                  _const_spec((1, d)), _const_spec((d, d)), _const_spec((1, d))],
        out_specs=pl.BlockSpec((1, CONV_TILE, d), lambda bi, i: (bi, i, 0)),
        out_shape=jax.ShapeDtypeStruct((b, s, d), F32),
        scratch_shapes=[pltpu.VMEM((CONV_TILE + CONV_HALO, d), F32),
                        pltpu.VMEM((SUBLANES - 1, CONV_TILE + CONV_HALO - SUBLANES, d), F32),
                        pltpu.VMEM((CONV_TILE, d), F32)],
        compiler_params=_params(2),
        name="conv_tail",
    )(u, u, h, jnp.broadcast_to(dw_w[:, None, :], (CONV_WIDTH, SUBLANES, d)), vec(dw_b),
      vec(ln_g), vec(ln_b), pw2_w.astype(BF16), vec(pw2_b))


def _lambda_init(layer_idx):
    return 0.8 - 0.6 * math.exp(-0.3 * layer_idx)


def kernel(x, p, attn_norm_g, w_qkv, lambda_q1, lambda_k1, lambda_q2, lambda_k2, subln_g, w_o, conv_norm_g, conv_pw1_w, conv_pw1_b, conv_dw_w, conv_dw_b, conv_ln_g, conv_ln_b, conv_pw2_w, conv_pw2_b, ffn_norm_g, w_gate, w_up, w_down, ple_norm_g, w_ple_gate, w_ple_proj, final_norm_g):
    b, s, d = x.shape
    depth = p.shape[0]
    t = b * s
    assert d == N_HEADS * V_HEAD_DIM and t % ROW_TILE == 0
    assert s % ATTN_Q == 0 and ATTN_Q % ATTN_K == 0 and ATTN_K % CHUNK == 0
    assert s % ROW_TILE == 0 and s % CONV_TILE == 0
    h = x.reshape(t, d)
    for i in range(depth):
        j = i // 2
        if i % 2 == 0:
            k, qvt = _norm_qkv(h, attn_norm_g[j], w_qkv[j], b, s)
            o = _attention(k.reshape(b, s, d), qvt, lambda_q1[j], lambda_k1[j], lambda_q2[j],
                           lambda_k2[j], subln_g[j], _lambda_init(i))
            proj = (o.reshape(t, d), w_o[j])
        else:
            u = _norm_glu(h, conv_norm_g[j], conv_pw1_w[j], conv_pw1_b[j])
            h = _conv_tail(u.reshape(b, s, d), h.reshape(b, s, d), conv_dw_w[j], conv_dw_b[j],
                           conv_ln_g[j], conv_ln_b[j], conv_pw2_w[j], conv_pw2_b[j]).reshape(t, d)
            proj = None
        h = _layer_tail(h, proj, ffn_norm_g[i], w_gate[i], w_up[i], w_down[i], p[i].reshape(t, -1),
                        ple_norm_g[i], w_ple_gate[i], w_ple_proj[i], final_norm_g,
                        final_norm=(i == depth - 1))
    return h.reshape(b, s, d)
```

```python
import functools
import math

import jax
import jax.numpy as jnp
from jax import lax
from jax.experimental import pallas as pl
from jax.experimental.pallas import tpu as pltpu

F32 = jnp.float32
BF16 = jnp.bfloat16

NORM_EPS = 1e-6
CHUNK = 64
N_HEADS = 8
HEAD_DIM = 64
V_HEAD_DIM = 2 * HEAD_DIM
CONV_WIDTH = 31
CONV_HALO = 32
MASK_VALUE = -1e30

SCORE_SCALE = HEAD_DIM ** -0.5 * math.log2(math.e)
SUM_ROWS = 16
SUBLANES = 8
MXU_COLS = 256
SOFTMAX_ROWS = 64
SOFTMAX_CHAINS = 2

ROW_TILE = 1024
ATTN_Q = 512
ATTN_K = 512
ATTN_UNROLL = 2
CONV_TILE = 1024
CONV_ROWS = 32
CONV_COLS = 512
COL_CHUNK = 256
FF_CHUNK = 256
VMEM_LIMIT = 56 * 1024 * 1024


def _params(n_axes):
    return pltpu.CompilerParams(dimension_semantics=("arbitrary",) * n_axes,
                                vmem_limit_bytes=VMEM_LIMIT)


def _rms(x, g):
    return x * lax.rsqrt(jnp.mean(x * x, axis=-1, keepdims=True) + NORM_EPS) * g


def _dot(a, b):
    return jnp.dot(a, b, preferred_element_type=F32)


def _const_spec(shape):
    return pl.BlockSpec(shape, lambda *_: (0,) * len(shape))


def _norm_qkv_kernel(x_ref, g_ref, wk_ref, wqvt_ref, k_ref, qvt_ref):
    xn = _rms(x_ref[...], g_ref[...]).astype(BF16)
    d = k_ref.shape[1]
    for c in range(0, d, COL_CHUNK):
        k_ref[:, c:c + COL_CHUNK] = _dot(xn, wk_ref[:, c:c + COL_CHUNK]).astype(k_ref.dtype)
    for c in range(0, 2 * d, COL_CHUNK):
        y = lax.dot_general(wqvt_ref[c:c + COL_CHUNK, :], xn, (((1,), (1,)), ((), ())),
                            preferred_element_type=F32)
        if c < d:
            y = y * SCORE_SCALE
        qvt_ref[0, c:c + COL_CHUNK, :] = y.astype(qvt_ref.dtype)


def _norm_qkv(h, g, w, b, s):
    t, d = h.shape
    per_seq = s // ROW_TILE
    w = w.astype(BF16)
    wqvt = jnp.concatenate([w[:, :d], w[:, 2 * d:]], axis=1).T
    return pl.pallas_call(
        _norm_qkv_kernel,
        grid=(t // ROW_TILE,),
        in_specs=[pl.BlockSpec((ROW_TILE, d), lambda i: (i, 0)),
                  _const_spec((1, d)), _const_spec((d, d)), _const_spec((2 * d, d))],
        out_specs=[pl.BlockSpec((ROW_TILE, d), lambda i: (i, 0)),
                   pl.BlockSpec((1, 2 * d, ROW_TILE), lambda i: (i // per_seq, 0, i % per_seq))],
        out_shape=[jax.ShapeDtypeStruct((t, d), BF16),
                   jax.ShapeDtypeStruct((b, 2 * d, s), BF16)],
        compiler_params=_params(1),
        name="norm_qkv",
    )(h, g.reshape(1, d), w[:, d:2 * d], wqvt)


def _norm_glu_kernel(x_ref, g_ref, w_ref, b_ref, o_ref):
    xn = _rms(x_ref[...], g_ref[...]).astype(BF16)
    n = o_ref.shape[1]
    for c in range(0, n, COL_CHUNK):
        a = _dot(xn, w_ref[:, c:c + COL_CHUNK]) + b_ref[:, c:c + COL_CHUNK]
        gt = _dot(xn, w_ref[:, n + c:n + c + COL_CHUNK]) + b_ref[:, n + c:n + c + COL_CHUNK]
        o_ref[:, c:c + COL_CHUNK] = a * jax.nn.sigmoid(gt)


def _norm_glu(h, g, w, b):
    t, d = h.shape
    n = w.shape[1] // 2
    return pl.pallas_call(
        _norm_glu_kernel,
        grid=(t // ROW_TILE,),
        in_specs=[pl.BlockSpec((ROW_TILE, d), lambda i: (i, 0)),
                  _const_spec((1, d)), _const_spec((d, 2 * n)), _const_spec((1, 2 * n))],
        out_specs=pl.BlockSpec((ROW_TILE, n), lambda i: (i, 0)),
        out_shape=jax.ShapeDtypeStruct((t, n), F32),
        compiler_params=_params(1),
        name="norm_glu",
    )(h, g.reshape(1, d), w.astype(BF16), b.reshape(1, 2 * n))


def _attn_kernel(qt_ref, k_ref, vt_ref, lq1_ref, lk1_ref, lq2_ref, lk2_ref, sg_ref, o_ref,
                 qq_ref, kx_ref, qx_ref, s0_ref, s1_ref, p0_ref, p1_ref, acc_ref, *, lambda_init):
    seq = k_ref.shape[1]
    tq, tk = ATTN_Q, ATTN_K
    n_q = seq // tq
    per_tile = tq // tk
    n_tasks = per_tile * n_q * (n_q + 1) // 2

    feature = lax.broadcasted_iota(jnp.int32, (V_HEAD_DIM, tq), 0)
    for i in range(n_q):
        q = qt_ref[0, :, i * tq:(i + 1) * tq]
        zero = jnp.zeros_like(q)
        qq_ref[i, :, :tq] = jnp.where(feature < HEAD_DIM, q, zero)
        qq_ref[i, :, tq:] = jnp.where(feature >= HEAD_DIM, q, zero)

    chunks = tk // CHUNK
    feat = lax.broadcasted_iota(jnp.int32, (tk, V_HEAD_DIM), 1)
    key_chunk = lax.broadcasted_iota(jnp.int32, (tk, V_HEAD_DIM), 0) // CHUNK
    kx_ref[...] = jnp.where(feat == key_chunk, 1.0, 0.0).astype(BF16)
    feat = lax.broadcasted_iota(jnp.int32, (V_HEAD_DIM, 2 * tq), 0)
    query_chunk = (lax.broadcasted_iota(jnp.int32, (V_HEAD_DIM, 2 * tq), 1) % tq) // CHUNK
    qx_ref[0] = jnp.zeros((V_HEAD_DIM, 2 * tq), BF16)
    for r in range(per_tile):
        hidden = (feat < chunks) & (feat + r * chunks > query_chunk)
        qx_ref[r + 1] = jnp.where(hidden, MASK_VALUE, 0.0).astype(BF16)

    ones = jnp.ones((SUM_ROWS, tk), BF16)
    lam = (jnp.exp(jnp.sum(lq1_ref[...] * lk1_ref[...], keepdims=True))
           - jnp.exp(jnp.sum(lq2_ref[...] * lk2_ref[...], keepdims=True)) + lambda_init)
    gain = sg_ref[...] * (1.0 - lambda_init)

    def scores(qi, ki, s_ref):
        qi = jnp.minimum(qi, n_q - 1)
        start = pl.multiple_of(jnp.minimum(ki, seq // tk - 1) * tk, tk)
        slot = jnp.clip(ki - qi * per_tile + 1, 0, per_tile)
        k_ext = jnp.concatenate([k_ref[0, pl.ds(start, tk), :], kx_ref[...]], axis=1)
        maxes = []
        for c in range(0, 2 * tq, MXU_COLS):
            q_ext = jnp.concatenate([qq_ref[qi, :, c:c + MXU_COLS],
                                     qx_ref[slot, :, c:c + MXU_COLS]], axis=0)
            s = _dot(k_ext, q_ext)
            s_ref[:, c:c + MXU_COLS] = s
            maxes.append(jnp.max(s.reshape(tk // SUBLANES, SUBLANES, MXU_COLS), axis=0))
        return jnp.concatenate(maxes, axis=1)

    def finalize(qi):
        buf = lax.rem(qi, 2)
        r = 1.0 / acc_ref[buf, V_HEAD_DIM:V_HEAD_DIM + 1, :]
        o = (acc_ref[buf, :V_HEAD_DIM, :tq] * r[:, :tq]
             - acc_ref[buf, :V_HEAD_DIM, tq:] * (lam * r[:, tq:]))
        o = o * lax.rsqrt(jnp.mean(o * o, axis=0, keepdims=True) + NORM_EPS) * gain
        o_ref[0, pl.ds(pl.multiple_of(qi * tq, tq), tq), :] = o.T.astype(o_ref.dtype)

    def sub_iteration(state, s_cur, s_nxt, p_cur):
        qi, ki, m, block_max = state
        n_blocks = (qi + 1) * per_tile
        last = ki == n_blocks - 1
        qi_next = jnp.where(last, qi + 1, qi)
        ki_next = jnp.where(last, 0, ki + 1)
        next_max = scores(qi_next, ki_next, s_nxt)
        m_old = jnp.where(ki == 0, MASK_VALUE, m)
        m_new = jnp.maximum(m_old, jnp.max(block_max, axis=0, keepdims=True))
        alpha = jnp.exp2(m_old - m_new)
        shifts = [jnp.broadcast_to(m_new, (SUBLANES, 2 * tq))] * SOFTMAX_CHAINS
        for g in range(tk // SOFTMAX_ROWS):
            rows = slice(g * SOFTMAX_ROWS, (g + 1) * SOFTMAX_ROWS)
            x = s_cur[rows, :].reshape(SOFTMAX_ROWS // SUBLANES, SUBLANES, 2 * tq)
            e = jnp.exp2(x - shifts[g % SOFTMAX_CHAINS]).reshape(SOFTMAX_ROWS, 2 * tq)
            p_cur[rows, :] = e.astype(BF16)
            shifts[g % SOFTMAX_CHAINS] = m_new + 0.0 * e[SOFTMAX_ROWS - SUBLANES:, :]
        start = pl.multiple_of(jnp.minimum(ki, seq // tk - 1) * tk, tk)
        vt_ext = jnp.concatenate([vt_ref[0, :, pl.ds(start, tk)], ones], axis=0)
        buf = lax.rem(qi, 2)
        for c in range(0, 2 * tq, MXU_COLS):
            cols = slice(c, c + MXU_COLS)
            acc_ref[buf, :, cols] = (alpha[:, cols] * acc_ref[buf, :, cols]
                                     + _dot(vt_ext, p_cur[:, cols]))
        done = jnp.where(last & (qi < n_q), qi, -1)
        return (qi_next, ki_next, m_new, next_max), done

    def body(_, state):
        assert ATTN_UNROLL == 2
        completed = []
        for _ in range(ATTN_UNROLL // 2):
            state, done = sub_iteration(state, s0_ref, s1_ref, p0_ref)
            completed.append(done)
            state, done = sub_iteration(state, s1_ref, s0_ref, p1_ref)
            completed.append(done)
        for done in completed:
            pl.when(done >= 0)(functools.partial(finalize, done))
        return state

    acc_ref[...] = jnp.zeros_like(acc_ref)
    zero = jnp.int32(0)
    first_max = scores(zero, zero, s0_ref)
    state = (zero, zero, jnp.full((1, 2 * tq), MASK_VALUE, F32), first_max)
    lax.fori_loop(0, pl.cdiv(n_tasks, ATTN_UNROLL), body, state)


def _attention(k, qvt, lq1, lk1, lq2, lk2, subln_g, lambda_init):
    b, s, d = k.shape
    vec = lambda a: a.reshape(1, -1)
    head_rows = pl.BlockSpec((1, s, V_HEAD_DIM), lambda bi, h: (bi, 0, h))
    head_cols = lambda row: pl.BlockSpec((1, V_HEAD_DIM, s), lambda bi, h: (bi, row + h, 0))
    return pl.pallas_call(
        functools.partial(_attn_kernel, lambda_init=lambda_init),
        grid=(b, N_HEADS),
        in_specs=[head_cols(0), head_rows, head_cols(N_HEADS),
                  _const_spec((1, HEAD_DIM)), _const_spec((1, HEAD_DIM)),
                  _const_spec((1, HEAD_DIM)), _const_spec((1, HEAD_DIM)),
                  _const_spec((V_HEAD_DIM, 1))],
        out_specs=head_rows,
        out_shape=jax.ShapeDtypeStruct((b, s, d), BF16),
        scratch_shapes=[pltpu.VMEM((s // ATTN_Q, V_HEAD_DIM, 2 * ATTN_Q), BF16),
                        pltpu.VMEM((ATTN_K, V_HEAD_DIM), BF16),
                        pltpu.VMEM((ATTN_Q // ATTN_K + 1, V_HEAD_DIM, 2 * ATTN_Q), BF16),
                        pltpu.VMEM((ATTN_K, 2 * ATTN_Q), F32),
                        pltpu.VMEM((ATTN_K, 2 * ATTN_Q), F32),
                        pltpu.VMEM((ATTN_K, 2 * ATTN_Q), BF16),
                        pltpu.VMEM((ATTN_K, 2 * ATTN_Q), BF16),
                        pltpu.VMEM((2, V_HEAD_DIM + SUM_ROWS, 2 * ATTN_Q), F32)],
        compiler_params=_params(2),
        name="diff_attention",
    )(qvt, k, qvt, vec(lq1), vec(lk1), vec(lq2), vec(lk2), subln_g.reshape(-1, 1))


def _tail_kernel(*refs, has_proj, final_norm):
    refs = list(refs)
    h_ref = refs.pop(0)
    if has_proj:
        a_ref, wo_ref = refs.pop(0), refs.pop(0)
    fg_ref, wg_ref, wu_ref, wd_ref, p_ref, pg_ref, wpg_ref, wpp_ref, fin_ref, o_ref = refs
    x = h_ref[...]
    if has_proj:
        x = x + _dot(a_ref[...], wo_ref[...])
    xn = _rms(x, fg_ref[...]).astype(BF16)
    o_ref[...] = x
    for c in range(0, wg_ref.shape[1], FF_CHUNK):
        gate = _dot(xn, wg_ref[:, c:c + FF_CHUNK])
        up = _dot(xn, wu_ref[:, c:c + FF_CHUNK])
        act = (gate * jax.nn.sigmoid(gate) * up).astype(BF16)
        o_ref[...] += _dot(act, wd_ref[c:c + FF_CHUNK, :])
    y = o_ref[...]
    yn = _rms(y, pg_ref[...]).astype(BF16)
    y = y + _dot(p_ref[...].astype(BF16), wpp_ref[...]) * jax.nn.sigmoid(_dot(yn, wpg_ref[...]))
    if final_norm:
        y = _rms(y, fin_ref[...])
    o_ref[...] = y


def _layer_tail(h, proj, ffn_g, w_gate, w_up, w_down, p, ple_g, w_ple_gate, w_ple_proj,
                final_g, final_norm):
    t, d = h.shape
    f = w_gate.shape[1]
    e = p.shape[1]
    rows = lambda n: pl.BlockSpec((ROW_TILE, n), lambda i: (i, 0))
    resident = lambda shape: pl.BlockSpec(shape, lambda i: (0, 0), pipeline_mode=pl.Buffered(1))
    vec = lambda a: a.reshape(1, d)
    args, specs = [h], [rows(d)]
    if proj is not None:
        args += [proj[0], proj[1].astype(BF16)]
        specs += [rows(d), resident((d, d))]
    args += [vec(ffn_g), w_gate.astype(BF16), w_up.astype(BF16), w_down.astype(BF16), p,
             vec(ple_g), w_ple_gate.astype(BF16), w_ple_proj.astype(BF16), vec(final_g)]
    specs += [_const_spec((1, d)), resident((d, f)), resident((d, f)), resident((f, d)), rows(e),
              _const_spec((1, d)), resident((d, d)), resident((e, d)), _const_spec((1, d))]
    return pl.pallas_call(
        functools.partial(_tail_kernel, has_proj=proj is not None, final_norm=final_norm),
        grid=(t // ROW_TILE,),
        in_specs=specs,
        out_specs=rows(d),
        out_shape=jax.ShapeDtypeStruct((t, d), F32),
        compiler_params=_params(1),
        name="layer_tail",
    )(*args)


def _conv_kernel(u_ref, halo_ref, h_ref, dw_ref, dwb_ref, lng_ref, lnb_ref, w_ref, b_ref,
                 o_ref, ext_ref, shift_ref, y_ref):
    tile, d = u_ref.shape[1], u_ref.shape[2]
    first = pl.program_id(1) == 0
    ext_ref[0:CONV_HALO, :] = jnp.where(first, 0.0, halo_ref[0])
    ext_ref[CONV_HALO:, :] = u_ref[0]
    span = shift_ref.shape[1]
    for r in range(1, SUBLANES):
        shift_ref[r - 1] = ext_ref[r:r + span, :]
    lead = CONV_HALO - (CONV_WIDTH - 1)
    groups = CONV_ROWS // SUBLANES
    for c in range(0, d, CONV_COLS):
        cols = slice(c, c + CONV_COLS)
        for r0 in range(0, tile, CONV_ROWS):
            acc = jnp.zeros((groups, SUBLANES, CONV_COLS), F32) + dwb_ref[:, cols]
            for k in range(CONV_WIDTH):
                r = (lead + k) % SUBLANES
                a = r0 + lead + k - r
                x = ext_ref[a:a + CONV_ROWS, cols] if r == 0 else shift_ref[r - 1, a:a + CONV_ROWS, cols]
                acc = acc + x.reshape(groups, SUBLANES, CONV_COLS) * dw_ref[k, :, cols]
            y_ref[r0:r0 + CONV_ROWS, cols] = acc.reshape(CONV_ROWS, CONV_COLS)
    y = y_ref[...]
    mu = jnp.mean(y, axis=-1, keepdims=True)
    yc = y - mu
    y = yc * lax.rsqrt(jnp.mean(yc * yc, axis=-1, keepdims=True) + NORM_EPS) * lng_ref[...] + lnb_ref[...]
    y = (y * jax.nn.sigmoid(y)).astype(BF16)
    o_ref[0] = h_ref[0] + _dot(y, w_ref[...]) + b_ref[...]


def _conv_tail(u, h, dw_w, dw_b, ln_g, ln_b, pw2_w, pw2_b):
    b, s, d = u.shape
    per_tile = CONV_TILE // CONV_HALO
    vec = lambda a: a.reshape(1, d)
    return pl.pallas_call(
        _conv_kernel,
        grid=(b, s // CONV_TILE),
        in_specs=[pl.BlockSpec((1, CONV_TILE, d), lambda bi, i: (bi, i, 0)),
                  pl.BlockSpec((1, CONV_HALO, d),
                               lambda bi, i: (bi, jnp.maximum(i * per_tile - 1, 0), 0)),
                  pl.BlockSpec((1, CONV_TILE, d), lambda bi, i: (bi, i, 0)),
                  _const_spec((CONV_WIDTH, SUBLANES, d)), _const_spec((1, d)), _const_spec((1, d)),
                  _const_spec((1, d)), _const_spec((d, d)), _const_spec((1, d))],
        out_specs=pl.BlockSpec((1, CONV_TILE, d), lambda bi, i: (bi, i, 0)),
        out_shape=jax.ShapeDtypeStruct((b, s, d), F32),
        scratch_shapes=[pltpu.VMEM((CONV_TILE + CONV_HALO, d), F32),
                        pltpu.VMEM((SUBLANES - 1, CONV_TILE + CONV_HALO - SUBLANES, d), F32),
                        pltpu.VMEM((CONV_TILE, d), F32)],
        compiler_params=_params(2),
        name="conv_tail",
    )(u, u, h, jnp.broadcast_to(dw_w[:, None, :], (CONV_WIDTH, SUBLANES, d)), vec(dw_b),
      vec(ln_g), vec(ln_b), pw2_w.astype(BF16), vec(pw2_b))


def _lambda_init(layer_idx):
    return 0.8 - 0.6 * math.exp(-0.3 * layer_idx)


def kernel(x, p, attn_norm_g, w_qkv, lambda_q1, lambda_k1, lambda_q2, lambda_k2, subln_g, w_o, conv_norm_g, conv_pw1_w, conv_pw1_b, conv_dw_w, conv_dw_b, conv_ln_g, conv_ln_b, conv_pw2_w, conv_pw2_b, ffn_norm_g, w_gate, w_up, w_down, ple_norm_g, w_ple_gate, w_ple_proj, final_norm_g):
    b, s, d = x.shape
    depth = p.shape[0]
    t = b * s
    assert d == N_HEADS * V_HEAD_DIM and t % ROW_TILE == 0
    assert s % ATTN_Q == 0 and ATTN_Q % ATTN_K == 0 and ATTN_K % CHUNK == 0
    assert s % ROW_TILE == 0 and s % CONV_TILE == 0
    h = x.reshape(t, d)
    for i in range(depth):
        j = i // 2
        if i % 2 == 0:
            k, qvt = _norm_qkv(h, attn_norm_g[j], w_qkv[j], b, s)
            o = _attention(k.reshape(b, s, d), qvt, lambda_q1[j], lambda_k1[j], lambda_q2[j],
                           lambda_k2[j], subln_g[j], _lambda_init(i))
            proj = (o.reshape(t, d), w_o[j])
        else:
            u = _norm_glu(h, conv_norm_g[j], conv_pw1_w[j], conv_pw1_b[j])
            h = _conv_tail(u.reshape(b, s, d), h.reshape(b, s, d), conv_dw_w[j], conv_dw_b[j],
                           conv_ln_g[j], conv_ln_b[j], conv_pw2_w[j], conv_pw2_b[j]).reshape(t, d)
            proj = None
        h = _layer_tail(h, proj, ffn_norm_g[i], w_gate[i], w_up[i], w_down[i], p[i].reshape(t, -1),
                        ple_norm_g[i], w_ple_gate[i], w_ple_proj[i], final_norm_g,
                        final_norm=(i == depth - 1))
    return h.reshape(b, s, d)
```
